```python
import math
import jax
import jax.numpy as jnp
from jax import lax
import numpy as np

D_MODEL = 4096
BATCH = 1
SEQ = 16384
DEPTH = 2

N_META = 16
CHUNK = 128
SSD_HEADDIM = 64
SSD_HEADS = D_MODEL // SSD_HEADDIM
SSD_INNER = SSD_HEADS * SSD_HEADDIM
SSD_GROUPS = 8
SSD_HEADS_PER_GROUP = SSD_HEADS // SSD_GROUPS
SSD_STATE = 128
SSD_CONV = 4
SSD_CONV_DIM = SSD_INNER + 2 * SSD_GROUPS * SSD_STATE
S5_GROUP = 16
S5_WIDTH = D_MODEL // 2
S5_GROUPS = S5_WIDTH // S5_GROUP
S5_STATE = 64
D_FF = -((-8 * D_MODEL) // (3 * 256)) * 256
IN_COLS = SSD_INNER + SSD_CONV_DIM + SSD_HEADS + S5_WIDTH + 2 * D_MODEL
OFF_XBC = SSD_INNER
OFF_DT = OFF_XBC + SSD_CONV_DIM
OFF_U = OFF_DT + SSD_HEADS
OFF_GA = OFF_U + S5_WIDTH
OFF_GB = OFF_GA + D_MODEL
EPS = 1e-6
DT_MIN = 1e-3
DT_MAX = 1e-1

kernel_name = "hybrid_ssd_s5_gated_block"


def _rmsnorm(t, w):
    tf = t.astype(jnp.float32)
    out = tf * lax.rsqrt(jnp.mean(tf * tf, axis=-1, keepdims=True) + EPS) * w.astype(jnp.float32)
    return out.astype(t.dtype)


def _to_chunks(t):
    pad = CHUNK - N_META
    t = jnp.pad(t, [(0, 0), (pad, 0)] + [(0, 0)] * (t.ndim - 2))
    b, lp = t.shape[0], t.shape[1]
    t = t.reshape((b, lp // CHUNK, CHUNK) + t.shape[2:])
    return jnp.moveaxis(t, 1, 0)


def _from_chunks(t):
    t = jnp.moveaxis(t, 0, 1)
    b, nc = t.shape[0], t.shape[1]
    t = t.reshape((b, nc * CHUNK) + t.shape[3:])
    return t[:, CHUNK - N_META:]


def _causal_dwconv(t, w, bias):
    k = w.shape[0]
    out = lax.conv_general_dilated(t, w[:, None, :], window_strides=(1,), padding=[(k - 1, 0)],
                                   dimension_numbers=("NWC", "WIO", "NWC"),
                                   feature_group_count=t.shape[-1])
    return out + bias


def _ssd_mixer(z, xbc, dt_raw, conv_w, conv_b, dt_bias, a_log, d_skip, norm_w):
    f32 = jnp.float32
    out_dtype = z.dtype
    b, l = z.shape[0], z.shape[1]
    G, J, P, N = SSD_GROUPS, SSD_HEADS_PER_GROUP, SSD_HEADDIM, SSD_STATE
    xbc = jax.nn.silu(_causal_dwconv(xbc.astype(f32), conv_w.astype(f32), conv_b.astype(f32)))
    xs = xbc[..., :SSD_INNER].reshape(b, l, G, J, P)
    bm = xbc[..., SSD_INNER:SSD_INNER + G * N].reshape(b, l, G, N)
    cm = xbc[..., SSD_INNER + G * N:].reshape(b, l, G, N)
    dt = jax.nn.softplus(dt_raw.astype(f32) + dt_bias.astype(f32)).reshape(b, l, G, J)
    a_head = -jnp.exp(a_log.astype(f32)).reshape(G, J)
    causal = jnp.tril(jnp.ones((CHUNK, CHUNK), dtype=bool))[None, :, :, None, None]

    def chunk_step(state, inp):
        x_c, dt_c, b_c, c_c = inp
        a_cs = jnp.cumsum(dt_c * a_head, axis=1)
        seg = a_cs[:, :, None] - a_cs[:, None, :]
        lmat = jnp.exp(jnp.where(causal, seg, -jnp.inf))
        cb = jnp.einsum("bqgn,bsgn->bqsg", c_c, b_c)
        xdt = x_c * dt_c[..., None]
        y_diag = jnp.einsum("bqsgj,bsgjp->bqgjp", cb[..., None] * lmat, xdt)
        y_off = jnp.einsum("bqgn,bgjpn->bqgjp", c_c, state) * jnp.exp(a_cs)[..., None]
        decay_end = jnp.exp(a_cs[:, -1:] - a_cs)
        new_state = (state * jnp.exp(a_cs[:, -1])[..., None, None]
                     + jnp.einsum("bsgn,bsgjp->bgjpn", b_c, xdt * decay_end[..., None]))
        return new_state, y_diag + y_off

    state0 = jnp.zeros((b, G, J, P, N), f32)
    _, ys = lax.scan(chunk_step, state0,
                     (_to_chunks(xs), _to_chunks(dt), _to_chunks(bm), _to_chunks(cm)))
    y = _from_chunks(ys) + xs * d_skip.astype(f32).reshape(G, J)[:, :, None]
    y = y.reshape(b, l, SSD_INNER) * jax.nn.silu(z.astype(f32))
    yg = y.reshape(b, l, SSD_GROUPS, SSD_INNER // SSD_GROUPS)
    yg = yg * lax.rsqrt(jnp.mean(yg * yg, axis=-1, keepdims=True) + EPS)
    return (yg.reshape(b, l, SSD_INNER) * norm_w.astype(f32)).astype(out_dtype)


def _complex_affine_combine(e1, e2):
    a1r, a1i, b1r, b1i = e1
    a2r, a2i, b2r, b2i = e2
    return (a2r * a1r - a2i * a1i,
            a2r * a1i + a2i * a1r,
            a2r * b1r - a2i * b1i + b2r,
            a2r * b1i + a2i * b1r + b2i)


def _s5_mixer(u, lam_re, lam_im, log_step, b_re, b_im, c_re, c_im, d_skip):
    f32 = jnp.float32
    out_dtype = u.dtype
    b, l = u.shape[0], u.shape[1]
    uf = u.astype(f32)
    lr, li = lam_re.astype(f32), lam_im.astype(f32)
    step = jnp.exp(log_step.astype(f32))[:, None]
    mag = jnp.exp(lr * step)
    ang = li * step
    lb_re, lb_im = mag * jnp.cos(ang), mag * jnp.sin(ang)
    denom = lr * lr + li * li
    nr, ni = lb_re - 1.0, lb_im
    f_re = (nr * lr + ni * li) / denom
    f_im = (ni * lr - nr * li) / denom
    br, bi = b_re.astype(f32), b_im.astype(f32)
    bb_re = f_re[..., None] * br - f_im[..., None] * bi
    bb_im = f_re[..., None] * bi + f_im[..., None] * br
    cr, ci = c_re.astype(f32), c_im.astype(f32)

    def chunk_step(carry, u_c):
        h_re, h_im = carry
        bu_re = jnp.einsum("bqgc,gpc->bqgp", u_c, bb_re)
        bu_im = jnp.einsum("bqgc,gpc->bqgp", u_c, bb_im)
        a_re = jnp.broadcast_to(lb_re, bu_re.shape)
        a_im = jnp.broadcast_to(lb_im, bu_im.shape)
        pa_re, pa_im, s_re, s_im = lax.associative_scan(
            _complex_affine_combine, (a_re, a_im, bu_re, bu_im), axis=1)
        st_re = s_re + pa_re * h_re[:, None] - pa_im * h_im[:, None]
        st_im = s_im + pa_re * h_im[:, None] + pa_im * h_re[:, None]
        y = jnp.einsum("bqgp,gcp->bqgc", st_re, cr) - jnp.einsum("bqgp,gcp->bqgc", st_im, ci)
        return (st_re[:, -1], st_im[:, -1]), y

    h0 = jnp.zeros((b, S5_GROUPS, S5_STATE), f32)
    _, ys = lax.scan(chunk_step, (h0, h0), _to_chunks(uf.reshape(b, l, S5_GROUPS, S5_GROUP)))
    y = _from_chunks(ys).reshape(b, l, S5_WIDTH) + d_skip.astype(f32) * uf
    return y.astype(out_dtype)


def setup_inputs(seed: int = 0) -> dict:
    key = jax.random.key(seed)
    ks = jax.random.split(key, 26)
    f32 = jnp.float32
    nrm = lambda k, s, sc: jax.random.normal(k, s, f32) * sc
    dt0 = jnp.exp(jax.random.uniform(ks[5], (DEPTH, SSD_HEADS), f32, math.log(DT_MIN), math.log(DT_MAX)))
    lam_im0 = jnp.pi * jnp.arange(S5_STATE, dtype=f32)
    return {
        "x": nrm(ks[0], (BATCH, SEQ, D_MODEL), 1.0),
        "meta_tokens": nrm(ks[1], (N_META, D_MODEL), 1.0),
        "norm_mix": 1.0 + nrm(ks[2], (DEPTH, D_MODEL), 0.02),
        "w_in": nrm(ks[3], (DEPTH, D_MODEL, IN_COLS), D_MODEL ** -0.5),
        "conv_w": nrm(ks[4], (DEPTH, SSD_CONV, SSD_CONV_DIM), SSD_CONV ** -0.5),
        "conv_b": nrm(ks[6], (DEPTH, SSD_CONV_DIM), 0.02),
        "dt_bias": dt0 + jnp.log(-jnp.expm1(-dt0)),
        "a_log": jnp.log(jax.random.uniform(ks[7], (DEPTH, SSD_HEADS), f32, 1.0, 16.0)),
        "d_ssd": 1.0 + nrm(ks[8], (DEPTH, SSD_HEADS), 0.1),
        "ssd_norm": 1.0 + nrm(ks[9], (DEPTH, SSD_INNER), 0.02),
        "w_ssd_up": nrm(ks[10], (DEPTH, SSD_INNER, D_MODEL), SSD_INNER ** -0.5),
        "lam_re": -0.5 + nrm(ks[11], (DEPTH, S5_GROUPS, S5_STATE), 0.01),
        "lam_im": lam_im0 + nrm(ks[12], (DEPTH, S5_GROUPS, S5_STATE), 0.01),
        "log_step": jax.random.uniform(ks[13], (DEPTH, S5_GROUPS), f32, math.log(DT_MIN), math.log(DT_MAX)),
        "b_re": nrm(ks[14], (DEPTH, S5_GROUPS, S5_STATE, S5_GROUP), (2 * S5_GROUP) ** -0.5),
        "b_im": nrm(ks[15], (DEPTH, S5_GROUPS, S5_STATE, S5_GROUP), (2 * S5_GROUP) ** -0.5),
        "c_re": nrm(ks[16], (DEPTH, S5_GROUPS, S5_GROUP, S5_STATE), S5_STATE ** -0.5),
        "c_im": nrm(ks[17], (DEPTH, S5_GROUPS, S5_GROUP, S5_STATE), S5_STATE ** -0.5),
        "d_s5": nrm(ks[18], (DEPTH, S5_WIDTH), 1.0),
        "w_glu": nrm(ks[19], (DEPTH, S5_WIDTH, 2 * D_MODEL), S5_WIDTH ** -0.5),
        "w_out": nrm(ks[20], (DEPTH, D_MODEL, D_MODEL), D_MODEL ** -0.5),
        "norm_ffn": 1.0 + nrm(ks[21], (DEPTH, D_MODEL), 0.02),
        "w_gate_up": nrm(ks[22], (DEPTH, D_MODEL, 2 * D_FF), D_MODEL ** -0.5),
        "w_down": nrm(ks[23], (DEPTH, D_FF, D_MODEL), D_FF ** -0.5),
        "norm_final": 1.0 + nrm(ks[24], (D_MODEL,), 0.02),
    }


def reference(x, meta_tokens, norm_mix, w_in, conv_w, conv_b, dt_bias, a_log, d_ssd, ssd_norm,
              w_ssd_up, lam_re, lam_im, log_step, b_re, b_im, c_re, c_im, d_s5, w_glu, w_out,
              norm_ffn, w_gate_up, w_down, norm_final):
    b = x.shape[0]
    meta = jnp.broadcast_to(meta_tokens[None].astype(x.dtype), (b, N_META, D_MODEL))
    h = jnp.concatenate([meta, x], axis=1)
    for l in range(DEPTH):
        hn = _rmsnorm(h, norm_mix[l])
        proj = hn @ w_in[l]
        z = proj[..., :OFF_XBC]
        xbc = proj[..., OFF_XBC:OFF_DT]
        dt_raw = proj[..., OFF_DT:OFF_U]
        u = proj[..., OFF_U:OFF_GA]
        gate_a = jax.nn.sigmoid(proj[..., OFF_GA:OFF_GB])
        gate_b = jax.nn.sigmoid(proj[..., OFF_GB:])
        y_a = _ssd_mixer(z, xbc, dt_raw, conv_w[l], conv_b[l], dt_bias[l], a_log[l], d_ssd[l], ssd_norm[l])
        branch_a = y_a @ w_ssd_up[l]
        y_b = _s5_mixer(u, lam_re[l], lam_im[l], log_step[l], b_re[l], b_im[l], c_re[l], c_im[l], d_s5[l])
        glu = jax.nn.gelu(y_b) @ w_glu[l]
        branch_b = glu[..., :D_MODEL] * jax.nn.sigmoid(glu[..., D_MODEL:])
        h = h + (gate_a * branch_a + gate_b * branch_b) @ w_out[l]
        hn = _rmsnorm(h, norm_ffn[l])
        gu = hn @ w_gate_up[l]
        h = h + (jax.nn.silu(gu[..., :D_FF]) * gu[..., D_FF:]) @ w_down[l]
    h = _rmsnorm(h, norm_final)
    return h[:, N_META:]
```

```python
import functools
import math

import jax
import jax.numpy as jnp
from jax import lax
from jax.experimental import pallas as pl
from jax.experimental.pallas import tpu as pltpu

F32 = jnp.float32
BF16 = jnp.bfloat16

N_META = 16
EPS = 1e-6
SSD_P = 64
SSD_N = 128
SSD_G = 8
SSD_J = 8
SSD_Q = 128
SSD_CONV = 4
GW = SSD_J * SSD_P
S5_C = 16
S5_P = 64
S5_Q = 16
S5_GB = 8
S5_ROW_BLOCKS = 3
LANES = 128
ROW_ALIGN = S5_Q * 8 * S5_ROW_BLOCKS
VMEM_LIMIT = 56 * 1024 * 1024


def _pick(n, cap, align=16):
    best = None
    for d in range(align, min(n, cap) + 1, align):
        if n % d == 0:
            best = d
    assert best is not None, (n, cap)
    return best


def _params(sem):
    return pltpu.CompilerParams(dimension_semantics=sem, vmem_limit_bytes=VMEM_LIMIT)


def _rmsnorm_kernel(x_ref, w_ref, o_ref):
    x = x_ref[...]
    ms = jnp.mean(x * x, axis=-1, keepdims=True)
    o_ref[...] = (x * lax.rsqrt(ms + EPS) * w_ref[...]).astype(o_ref.dtype)


def _rmsnorm(h, w, out_dtype, name):
    m, d = h.shape
    bm = _pick(m, 384)
    return pl.pallas_call(
        _rmsnorm_kernel,
        grid=(m // bm,),
        in_specs=[pl.BlockSpec((bm, d), lambda i: (i, 0)),
                  pl.BlockSpec((1, d), lambda i: (0, 0))],
        out_specs=pl.BlockSpec((bm, d), lambda i: (i, 0)),
        out_shape=jax.ShapeDtypeStruct((m, d), out_dtype),
        compiler_params=_params(("arbitrary",)),
        name=name,
    )(h, w.reshape(1, d))


def _mm_kernel(x_ref, w_ref, o_ref, *, act):
    acc = jnp.dot(x_ref[...], w_ref[...], preferred_element_type=F32)
    if act == "sigmoid":
        acc = jax.nn.sigmoid(acc)
    o_ref[...] = acc.astype(o_ref.dtype)


def _mm(x, w, out_dtype, name, act=None, bm_cap=1376, bn_cap=512):
    m, k = x.shape
    n = w.shape[1]
    bm = _pick(m, bm_cap)
    bn = _pick(n, bn_cap, LANES)
    return pl.pallas_call(
        functools.partial(_mm_kernel, act=act),
        grid=(m // bm, n // bn),
        in_specs=[pl.BlockSpec((bm, k), lambda i, j: (i, 0)),
                  pl.BlockSpec((k, bn), lambda i, j: (0, j))],
        out_specs=pl.BlockSpec((bm, bn), lambda i, j: (i, j)),
        out_shape=jax.ShapeDtypeStruct((m, n), out_dtype),
        compiler_params=_params(("arbitrary", "arbitrary")),
        name=name,
    )(x, w)


def _mm_res_kernel(x_ref, w_ref, r_ref, o_ref):
    o_ref[...] = r_ref[...] + jnp.dot(x_ref[...], w_ref[...], preferred_element_type=F32)


def _mm_res(x, w, res, name, bm_cap, bn_cap):
    m, k = x.shape
    n = w.shape[1]
    bm = _pick(m, bm_cap)
    bn = _pick(n, bn_cap, LANES)
    return pl.pallas_call(
        _mm_res_kernel,
        grid=(m // bm, n // bn),
        in_specs=[pl.BlockSpec((bm, k), lambda i, j: (i, 0)),
                  pl.BlockSpec((k, bn), lambda i, j: (0, j)),
                  pl.BlockSpec((bm, bn), lambda i, j: (i, j))],
        out_specs=pl.BlockSpec((bm, bn), lambda i, j: (i, j)),
        out_shape=jax.ShapeDtypeStruct((m, n), F32),
        input_output_aliases={2: 0},
        compiler_params=_params(("arbitrary", "arbitrary")),
        name=name,
    )(x, w, res)


def _swiglu_kernel(x_ref, wg_ref, wu_ref, o_ref):
    x = x_ref[...]
    g = jnp.dot(x, wg_ref[...], preferred_element_type=F32)
    u = jnp.dot(x, wu_ref[...], preferred_element_type=F32)
    o_ref[...] = (g * jax.nn.sigmoid(g) * u).astype(o_ref.dtype)


def _swiglu(x, w_gate_up, name):
    m, k = x.shape
    f = w_gate_up.shape[1] // 2
    bm = _pick(m, 1376)
    bn = _pick(f, 512, LANES)
    nb = f // bn
    return pl.pallas_call(
        _swiglu_kernel,
        grid=(m // bm, nb),
        in_specs=[pl.BlockSpec((bm, k), lambda i, j: (i, 0)),
                  pl.BlockSpec((k, bn), lambda i, j: (0, j)),
                  pl.BlockSpec((k, bn), lambda i, j: (0, j + nb))],
        out_specs=pl.BlockSpec((bm, bn), lambda i, j: (i, j)),
        out_shape=jax.ShapeDtypeStruct((m, f), BF16),
        compiler_params=_params(("arbitrary", "arbitrary")),
        name=name,
    )(x, w_gate_up, w_gate_up)


def _merge_kernel(ya_ref, gy_ref, wup_ref, wg1_ref, wg2_ref, ga_ref, gb_ref, o_ref):
    a = jnp.dot(ya_ref[...], wup_ref[...], preferred_element_type=F32)
    gy = gy_ref[...]
    g1 = jnp.dot(gy, wg1_ref[...], preferred_element_type=F32)
    g2 = jnp.dot(gy, wg2_ref[...], preferred_element_type=F32)
    o_ref[...] = (ga_ref[...] * a + gb_ref[...] * (g1 * jax.nn.sigmoid(g2))).astype(o_ref.dtype)


def _merge(ya, gy, w_up, w_glu, gates, name):
    m, d = ya.shape
    k2 = gy.shape[1]
    bm = _pick(m, 688)
    bn = _pick(d, 512, LANES)
    nb = d // bn
    return pl.pallas_call(
        _merge_kernel,
        grid=(m // bm, nb),
        in_specs=[pl.BlockSpec((bm, d), lambda i, j: (i, 0)),
                  pl.BlockSpec((bm, k2), lambda i, j: (i, 0)),
                  pl.BlockSpec((d, bn), lambda i, j: (0, j)),
                  pl.BlockSpec((k2, bn), lambda i, j: (0, j)),
                  pl.BlockSpec((k2, bn), lambda i, j: (0, j + nb)),
                  pl.BlockSpec((bm, bn), lambda i, j: (i, j)),
                  pl.BlockSpec((bm, bn), lambda i, j: (i, j + nb))],
        out_specs=pl.BlockSpec((bm, bn), lambda i, j: (i, j)),
        out_shape=jax.ShapeDtypeStruct((m, d), BF16),
        compiler_params=_params(("arbitrary", "arbitrary")),
        name=name,
    )(ya, gy, w_up, w_glu, w_glu, gates, gates)


def _softplus(x):
    return jnp.maximum(x, 0.0) + jnp.log1p(jnp.exp(-jnp.abs(x)))


def _ssd_kernel(z_ref, xbc_ref, dt_ref, cw_ref, cb_ref, dtb_ref, alog_ref, dsk_ref, nw_ref,
                y_ref,
                ext_ref, xs_ref, bm_ref, cm_ref, st_ref, ya_ref, acsg_ref, acst_ref, dtt_ref):
    q = SSD_Q
    c = pl.program_id(0)

    @pl.when(c == 0)
    def _():
        ext_ref[0:8, :] = jnp.zeros((8, ext_ref.shape[1]), F32)
        st_ref[...] = jnp.zeros_like(st_ref)

    ext_ref[8:8 + q, :] = xbc_ref[...]
    n_x = SSD_G * GW
    for blk in range(ext_ref.shape[1] // GW):
        cols = slice(blk * GW, (blk + 1) * GW)
        acc = cb_ref[:, cols] + cw_ref[3:4, cols] * ext_ref[8:8 + q, cols]
        for k in range(SSD_CONV - 1):
            acc = acc + cw_ref[k:k + 1, cols] * ext_ref[5 + k:5 + k + q, cols]
        act = acc * jax.nn.sigmoid(acc)
        if blk < SSD_G:
            xs_ref[blk] = act
        else:
            for i in range(GW // SSD_N):
                gi = (blk - SSD_G) * (GW // SSD_N) + i
                piece = act[:, i * SSD_N:(i + 1) * SSD_N]
                if gi < SSD_G:
                    bm_ref[gi] = piece
                else:
                    cm_ref[gi - SSD_G] = piece
    ext_ref[0:8, :] = ext_ref[q:q + 8, :]
    del n_x

    dt = _softplus(dt_ref[...] + dtb_ref[...])
    acs = dt * (-jnp.exp(alog_ref[...]))
    rows = lax.broadcasted_iota(jnp.int32, (q, LANES), 0)
    lanes = lax.broadcasted_iota(jnp.int32, (q, LANES), 1)
    sh = 1
    while sh < q:
        acs = acs + jnp.where(rows >= sh, pltpu.roll(acs, sh, axis=0), 0.0)
        sh *= 2
    acst_ref[...] = acs.T
    dtt_ref[...] = dt.T
    for g in range(SSD_G):
        acsg_ref[g] = acs if g == 0 else pltpu.roll(acs, LANES - SSD_J * g, axis=1)
    causal = rows >= lanes

    def group_body(g, carry):
        b_g = bm_ref[g]
        c_g = cm_ref[g]
        cb = lax.dot_general(c_g.astype(BF16), b_g.astype(BF16), (((1,), (1,)), ((), ())),
                             preferred_element_type=F32)
        b_t = b_g.T
        a_blk = acsg_ref[g]
        g8 = pl.multiple_of(g * SSD_J, SSD_J)
        a_t = acst_ref[pl.ds(g8, SSD_J), :]
        d_t = dtt_ref[pl.ds(g8, SSD_J), :]
        x_g = xs_ref[g]
        s_g = st_ref[g]
        ys, ns = [], []
        for j in range(SSD_J):
            col = a_blk[:, j:j + 1]
            row = a_t[j:j + 1, :]
            drow = d_t[j:j + 1, :]
            lm = jnp.where(causal, jnp.exp(col - row), 0.0) * (cb * drow)
            em = jnp.exp(col) * c_g
            lhs = jnp.concatenate([lm, em], axis=1).astype(BF16)
            x_h = x_g[:, j * SSD_P:(j + 1) * SSD_P]
            s_h = s_g[:, j * SSD_P:(j + 1) * SSD_P]
            rhs = jnp.concatenate([x_h, s_h], axis=0).astype(BF16)
            ys.append(jnp.dot(lhs, rhs, preferred_element_type=F32))
            last = row[:, q - 1:q]
            wrow = drow * jnp.exp(last - row)
            upd = jnp.dot((b_t * wrow).astype(BF16), x_h.astype(BF16), preferred_element_type=F32)
            ns.append(s_h * jnp.exp(last) + upd)
        ya_ref[g] = jnp.concatenate(ys, axis=1)
        st_ref[g] = jnp.concatenate(ns, axis=1)
        return carry

    lax.fori_loop(0, SSD_G, group_body, 0)

    for g in range(SSD_G):
        cols = slice(g * GW, (g + 1) * GW)
        y = ya_ref[g] + xs_ref[g] * dsk_ref[:, cols]
        zz = z_ref[:, cols]
        y = y * (zz * jax.nn.sigmoid(zz))
        ms = jnp.mean(y * y, axis=-1, keepdims=True)
        y_ref[:, cols] = (y * lax.rsqrt(ms + EPS) * nw_ref[:, cols]).astype(y_ref.dtype)


def _ssd(z, xbc, dt, conv_w, conv_b, dt_bias, a_log, d_ssd, ssd_norm, name):
    m, inner = z.shape
    cdim = xbc.shape[1]
    heads = dt_bias.shape[0]
    q = SSD_Q
    pad = LANES - heads
    dtb = jnp.pad(dt_bias, (0, pad)).reshape(1, LANES)
    alog = jnp.pad(a_log, (0, pad)).reshape(1, LANES)
    dsk = jnp.repeat(d_ssd, SSD_P).reshape(1, inner)
    full = lambda r, w: pl.BlockSpec((r, w), lambda c: (0, 0))
    return pl.pallas_call(
        _ssd_kernel,
        grid=(m // q,),
        in_specs=[pl.BlockSpec((q, inner), lambda c: (c, 0)),
                  pl.BlockSpec((q, cdim), lambda c: (c, 0)),
                  pl.BlockSpec((q, LANES), lambda c: (c, 0)),
                  full(SSD_CONV, cdim), full(1, cdim), full(1, LANES), full(1, LANES),
                  full(1, inner), full(1, inner)],
        out_specs=pl.BlockSpec((q, inner), lambda c: (c, 0)),
        out_shape=jax.ShapeDtypeStruct((m, inner), BF16),
        scratch_shapes=[pltpu.VMEM((q + 8, cdim), F32),
                        pltpu.VMEM((SSD_G, q, GW), F32),
                        pltpu.VMEM((SSD_G, q, SSD_N), F32),
                        pltpu.VMEM((SSD_G, q, SSD_N), F32),
                        pltpu.VMEM((SSD_G, SSD_N, GW), F32),
                        pltpu.VMEM((SSD_G, q, GW), F32),
                        pltpu.VMEM((SSD_G, q, LANES), F32),
                        pltpu.VMEM((LANES, q), F32),
                        pltpu.VMEM((LANES, q), F32)],
        compiler_params=_params(("arbitrary",)),
        name=name,
    )(z, xbc, dt, conv_w, conv_b.reshape(1, cdim), dtb, alog, dsk, ssd_norm.reshape(1, inner))


def _gelu_tanh(x):
    return 0.5 * x * (1.0 + jnp.tanh(math.sqrt(2.0 / math.pi) * (x + 0.044715 * (x * x * x))))


def _s5_kernel(u_ref, t_ref, w_ref, v_ref, lq_ref, dsk_ref, o_ref,
               ucat_ref, s_ref, h_ref, yscr_ref):
    nc = ucat_ref.shape[0]
    half = s_ref.shape[1] // 2

    @pl.when(pl.program_id(1) == 0)
    def _():
        h_ref[...] = jnp.zeros_like(h_ref)

    for s in range(S5_Q):
        ucat_ref[:, s * LANES:(s + 1) * LANES] = u_ref[pl.ds(s, nc, stride=S5_Q), :].astype(BF16)
    ucat = ucat_ref[...]
    s_ref[...] = jnp.dot(ucat, w_ref[...], preferred_element_type=F32)
    l_re = lq_ref[:, :half]
    l_im = lq_ref[:, half:]

    def step(c, carry):
        h_re, h_im = carry
        srow = s_ref[pl.ds(c, 1), :]
        s_ref[pl.ds(c, 1), :] = jnp.concatenate([h_re, h_im], axis=1)
        n_re = l_re * h_re - l_im * h_im + srow[:, :half]
        n_im = l_re * h_im + l_im * h_re + srow[:, half:]
        return n_re, n_im

    h_re, h_im = lax.fori_loop(0, nc, step, (h_ref[0:1, :half], h_ref[0:1, half:]))
    h_ref[0:1, :half] = h_re
    h_ref[0:1, half:] = h_im

    y = (jnp.dot(ucat, t_ref[...], preferred_element_type=F32)
         + jnp.dot(s_ref[...].astype(BF16), v_ref[...], preferred_element_type=F32))
    for s in range(S5_Q):
        us = u_ref[pl.ds(s, nc, stride=S5_Q), :]
        yb = y[:, s * LANES:(s + 1) * LANES] + dsk_ref[...] * us
        yscr_ref[pl.ds(s, nc, stride=S5_Q), :] = _gelu_tanh(yb)
    o_ref[...] = yscr_ref[...].astype(o_ref.dtype)


def _s5_matrices(lam_re, lam_im, log_step, b_re, b_im, c_re, c_im):
    hp = lax.Precision.HIGHEST
    g = lam_re.shape[0]
    nb = g // S5_GB
    qn = S5_Q
    step = jnp.exp(log_step)[:, None]
    mag = jnp.exp(lam_re * step)
    ang = lam_im * step
    lb_re, lb_im = mag * jnp.cos(ang), mag * jnp.sin(ang)
    denom = lam_re * lam_re + lam_im * lam_im
    nr, ni = lb_re - 1.0, lb_im
    f_re = (nr * lam_re + ni * lam_im) / denom
    f_im = (ni * lam_re - nr * lam_im) / denom
    bb_re = f_re[..., None] * b_re - f_im[..., None] * b_im
    bb_im = f_re[..., None] * b_im + f_im[..., None] * b_re
    pw_re = [jnp.ones_like(lb_re)]
    pw_im = [jnp.zeros_like(lb_im)]
    for _ in range(qn):
        r, i = pw_re[-1], pw_im[-1]
        pw_re.append(r * lb_re - i * lb_im)
        pw_im.append(r * lb_im + i * lb_re)
    pw_re = jnp.stack(pw_re)
    pw_im = jnp.stack(pw_im)
    eye = jnp.eye(S5_GB, dtype=F32)
    cl_re = c_re[None] * pw_re[:qn, :, None, :] - c_im[None] * pw_im[:qn, :, None, :]
    cl_im = c_re[None] * pw_im[:qn, :, None, :] + c_im[None] * pw_re[:qn, :, None, :]
    kt = (jnp.einsum("tgop,gpc->tgoc", cl_re, bb_re, precision=hp)
          - jnp.einsum("tgop,gpc->tgoc", cl_im, bb_im, precision=hp))
    si = jnp.arange(qn)
    tau = si[None, :] - si[:, None]
    ktoe = jnp.where((tau >= 0)[:, :, None, None, None], kt[jnp.clip(tau, 0, qn - 1)], 0.0)
    ktoe = ktoe.reshape(qn, qn, nb, S5_GB, S5_C, S5_C)
    t_mat = jnp.einsum("stbaoc,ad->bsactdo", ktoe, eye).reshape(nb, qn * LANES, qn * LANES)
    wp_re = pw_re[:qn][::-1][:, :, :, None] * bb_re[None] - pw_im[:qn][::-1][:, :, :, None] * bb_im[None]
    wp_im = pw_re[:qn][::-1][:, :, :, None] * bb_im[None] + pw_im[:qn][::-1][:, :, :, None] * bb_re[None]
    w_ri = jnp.stack([wp_re, wp_im]).reshape(2, qn, nb, S5_GB, S5_P, S5_C)
    w_mat = jnp.einsum("rsbapc,ad->bsacrdp", w_ri, eye).reshape(nb, qn * LANES, 2 * S5_GB * S5_P)
    m_re = c_re[None] * pw_re[1:, :, None, :] - c_im[None] * pw_im[1:, :, None, :]
    m_im = c_re[None] * pw_im[1:, :, None, :] + c_im[None] * pw_re[1:, :, None, :]
    v_ri = jnp.stack([m_re, -m_im]).reshape(2, qn, nb, S5_GB, S5_C, S5_P)
    v_mat = jnp.einsum("rtbaop,ad->braptdo", v_ri, eye).reshape(nb, 2 * S5_GB * S5_P, qn * LANES)
    lq = jnp.concatenate([pw_re[qn].reshape(nb, 1, S5_GB * S5_P),
                          pw_im[qn].reshape(nb, 1, S5_GB * S5_P)], axis=-1)
    return t_mat.astype(BF16), w_mat.astype(BF16), v_mat.astype(BF16), lq


def _s5(u, mats, d_s5, name):
    m, width = u.shape
    t_mat, w_mat, v_mat, lq = mats
    nb = width // LANES
    rows = m // S5_ROW_BLOCKS
    nc = rows // S5_Q
    kq = S5_Q * LANES
    ks = w_mat.shape[2]
    return pl.pallas_call(
        _s5_kernel,
        grid=(nb, S5_ROW_BLOCKS),
        in_specs=[pl.BlockSpec((rows, LANES), lambda b, r: (r, b)),
                  pl.BlockSpec((None, kq, kq), lambda b, r: (b, 0, 0)),
                  pl.BlockSpec((None, kq, ks), lambda b, r: (b, 0, 0)),
                  pl.BlockSpec((None, ks, kq), lambda b, r: (b, 0, 0)),
                  pl.BlockSpec((None, 1, ks), lambda b, r: (b, 0, 0)),
                  pl.BlockSpec((1, LANES), lambda b, r: (0, b))],
        out_specs=pl.BlockSpec((rows, LANES), lambda b, r: (r, b)),
        out_shape=jax.ShapeDtypeStruct((m, width), BF16),
        scratch_shapes=[pltpu.VMEM((nc, kq), BF16),
                        pltpu.VMEM((nc, ks), F32),
                        pltpu.VMEM((8, ks), F32),
                        pltpu.VMEM((rows, LANES), F32)],
        compiler_params=_params(("arbitrary", "arbitrary")),
        name=name,
    )(u, t_mat, w_mat, v_mat, lq, d_s5.reshape(1, width))


def kernel(x, meta_tokens, norm_mix, w_in, conv_w, conv_b, dt_bias, a_log, d_ssd, ssd_norm,
           w_ssd_up, lam_re, lam_im, log_step, b_re, b_im, c_re, c_im, d_s5, w_glu, w_out,
           norm_ffn, w_gate_up, w_down, norm_final):
    batch, seq, d = x.shape
    depth = w_in.shape[0]
    heads = dt_bias.shape[1]
    inner = heads * SSD_P
    cdim = conv_w.shape[2]
    s5w = d_s5.shape[1]
    off_xbc, off_dt = inner, inner + cdim
    off_u = off_dt + heads
    off_g = off_u + s5w
    length = N_META + seq
    lp = -(-length // ROW_ALIGN) * ROW_ALIGN

    outs = []
    for b in range(batch):
        h = jnp.concatenate([meta_tokens.astype(F32), x[b], jnp.zeros((lp - length, d), F32)], axis=0)
        for l in range(depth):
            wl = w_in[l]
            w_z = wl[:, :off_xbc].astype(BF16)
            w_xbc = wl[:, off_xbc:off_dt].astype(BF16)
            w_dt = jnp.pad(wl[:, off_dt:off_u], ((0, 0), (0, LANES - heads))).astype(BF16)
            w_u = wl[:, off_u:off_g].astype(BF16)
            w_g = wl[:, off_g:].astype(BF16)

            hn = _rmsnorm(h, norm_mix[l], BF16, f"norm_mix{l}")
            z = _mm(hn, w_z, F32, f"proj_z{l}")
            xbc = _mm(hn, w_xbc, F32, f"proj_xbc{l}")
            dt = _mm(hn, w_dt, F32, f"proj_dt{l}")
            u = _mm(hn, w_u, F32, f"proj_u{l}")
            gates = _mm(hn, w_g, F32, f"proj_gates{l}", act="sigmoid")

            y_a = _ssd(z, xbc, dt, conv_w[l], conv_b[l], dt_bias[l], a_log[l], d_ssd[l], ssd_norm[l],
                       f"ssd{l}")
            mats = _s5_matrices(lam_re[l], lam_im[l], log_step[l], b_re[l], b_im[l], c_re[l], c_im[l])
            gy = _s5(u, mats, d_s5[l], f"s5{l}")

            merged = _merge(y_a, gy, w_ssd_up[l].astype(BF16), w_glu[l].astype(BF16), gates, f"merge{l}")
            h = _mm_res(merged, w_out[l].astype(BF16), h, f"out_proj{l}", 1376, 512)

            hn = _rmsnorm(h, norm_ffn[l], BF16, f"norm_ffn{l}")
            hidden = _swiglu(hn, w_gate_up[l].astype(BF16), f"swiglu{l}")
            h = _mm_res(hidden, w_down[l].astype(BF16), h, f"down_proj{l}", 688, 256)
        out = _rmsnorm(h, norm_final, x.dtype, "norm_final")
        outs.append(out[N_META:length])
    return jnp.stack(outs, axis=0)
```

```python
import functools
import math

import jax
import jax.numpy as jnp
from jax import lax
from jax.experimental import pallas as pl
from jax.experimental.pallas import tpu as pltpu

F32 = jnp.float32
BF16 = jnp.bfloat16

N_META = 16
EPS = 1e-6
SSD_P = 64
SSD_N = 128
SSD_G = 8
SSD_J = 8
SSD_Q = 128
SSD_CONV = 4
GW = SSD_J * SSD_P
S5_C = 16
S5_P = 64
S5_Q = 16
S5_GB = 8
S5_ROW_BLOCKS = 3
LANES = 128
ROW_ALIGN = S5_Q * 8 * S5_ROW_BLOCKS
VMEM_LIMIT = 56 * 1024 * 1024


def _pick(n, cap, align=16):
    best = None
    for d in range(align, min(n, cap) + 1, align):
        if n % d == 0:
            best = d
    assert best is not None, (n, cap)
    return best


def _params(sem):
    return pltpu.CompilerParams(dimension_semantics=sem, vmem_limit_bytes=VMEM_LIMIT)


def _rmsnorm_kernel(x_ref, w_ref, o_ref):
    x = x_ref[...]
    ms = jnp.mean(x * x, axis=-1, keepdims=True)
    o_ref[...] = (x * lax.rsqrt(ms + EPS) * w_ref[...]).astype(o_ref.dtype)


def _rmsnorm(h, w, out_dtype, name):
    m, d = h.shape
    bm = _pick(m, 384)
    return pl.pallas_call(
        _rmsnorm_kernel,
        grid=(m // bm,),
        in_specs=[pl.BlockSpec((bm, d), lambda i: (i, 0)),
                  pl.BlockSpec((1, d), lambda i: (0, 0))],
        out_specs=pl.BlockSpec((bm, d), lambda i: (i, 0)),
        out_shape=jax.ShapeDtypeStruct((m, d), out_dtype),
        compiler_params=_params(("arbitrary",)),
        name=name,
    )(h, w.reshape(1, d))


def _mm_kernel(x_ref, w_ref, o_ref, *, act):
    acc = jnp.dot(x_ref[...], w_ref[...].astype(BF16), preferred_element_type=F32)
    if act == "sigmoid":
        acc = jax.nn.sigmoid(acc)
    o_ref[...] = acc.astype(o_ref.dtype)


def _w_spec(w, k, bn, layer, col0):
    if layer is None:
        return pl.BlockSpec((k, bn), lambda i, j: (0, j))
    assert col0 % bn == 0, (col0, bn)
    return pl.BlockSpec((None, k, bn), lambda i, j: (layer, 0, col0 // bn + j))


def _mm(x, w, out_dtype, name, act=None, bm_cap=1376, bn_cap=512, layer=None, col0=0, n=None):
    m, k = x.shape
    n = w.shape[1] if layer is None else n
    bm = _pick(m, bm_cap)
    bn = _pick(n, bn_cap, LANES)
    return pl.pallas_call(
        functools.partial(_mm_kernel, act=act),
        grid=(m // bm, n // bn),
        in_specs=[pl.BlockSpec((bm, k), lambda i, j: (i, 0)),
                  _w_spec(w, k, bn, layer, col0)],
        out_specs=pl.BlockSpec((bm, bn), lambda i, j: (i, j)),
        out_shape=jax.ShapeDtypeStruct((m, n), out_dtype),
        compiler_params=_params(("arbitrary", "arbitrary")),
        name=name,
    )(x, w)


def _mm_res_kernel(x_ref, w_ref, r_ref, o_ref):
    o_ref[...] = r_ref[...] + jnp.dot(x_ref[...], w_ref[...].astype(BF16), preferred_element_type=F32)


def _mm_res(x, w, res, name, bm_cap, bn_cap, layer=None):
    m, k = x.shape
    n = w.shape[-1]
    bm = _pick(m, bm_cap)
    bn = _pick(n, bn_cap, LANES)
    return pl.pallas_call(
        _mm_res_kernel,
        grid=(m // bm, n // bn),
        in_specs=[pl.BlockSpec((bm, k), lambda i, j: (i, 0)),
                  _w_spec(w, k, bn, layer, 0),
                  pl.BlockSpec((bm, bn), lambda i, j: (i, j))],
        out_specs=pl.BlockSpec((bm, bn), lambda i, j: (i, j)),
        out_shape=jax.ShapeDtypeStruct((m, n), F32),
        input_output_aliases={2: 0},
        compiler_params=_params(("arbitrary", "arbitrary")),
        name=name,
    )(x, w, res)


def _swiglu_kernel(x_ref, wg_ref, wu_ref, o_ref):
    x = x_ref[...]
    g = jnp.dot(x, wg_ref[...].astype(BF16), preferred_element_type=F32)
    u = jnp.dot(x, wu_ref[...].astype(BF16), preferred_element_type=F32)
    o_ref[...] = (g * jax.nn.sigmoid(g) * u).astype(o_ref.dtype)


def _swiglu(x, w_gate_up, layer, name):
    m, k = x.shape
    f = w_gate_up.shape[2] // 2
    bm = _pick(m, 1376)
    bn = _pick(f, 512, LANES)
    nb = f // bn
    return pl.pallas_call(
        _swiglu_kernel,
        grid=(m // bm, nb),
        in_specs=[pl.BlockSpec((bm, k), lambda i, j: (i, 0)),
                  pl.BlockSpec((None, k, bn), lambda i, j: (layer, 0, j)),
                  pl.BlockSpec((None, k, bn), lambda i, j: (layer, 0, j + nb))],
        out_specs=pl.BlockSpec((bm, bn), lambda i, j: (i, j)),
        out_shape=jax.ShapeDtypeStruct((m, f), BF16),
        compiler_params=_params(("arbitrary", "arbitrary")),
        name=name,
    )(x, w_gate_up, w_gate_up)


def _merge_kernel(ya_ref, gy_ref, wup_ref, wg1_ref, wg2_ref, ga_ref, gb_ref, o_ref):
    a = jnp.dot(ya_ref[...], wup_ref[...], preferred_element_type=F32)
    gy = gy_ref[...]
    g1 = jnp.dot(gy, wg1_ref[...], preferred_element_type=F32)
    g2 = jnp.dot(gy, wg2_ref[...], preferred_element_type=F32)
    o_ref[...] = (ga_ref[...] * a + gb_ref[...] * (g1 * jax.nn.sigmoid(g2))).astype(o_ref.dtype)


def _merge(ya, gy, w_up, w_glu, gates, name):
    m, d = ya.shape
    k2 = gy.shape[1]
    bm = _pick(m, 688)
    bn = _pick(d, 512, LANES)
    nb = d // bn
    return pl.pallas_call(
        _merge_kernel,
        grid=(m // bm, nb),
        in_specs=[pl.BlockSpec((bm, d), lambda i, j: (i, 0)),
                  pl.BlockSpec((bm, k2), lambda i, j: (i, 0)),
                  pl.BlockSpec((d, bn), lambda i, j: (0, j)),
                  pl.BlockSpec((k2, bn), lambda i, j: (0, j)),
                  pl.BlockSpec((k2, bn), lambda i, j: (0, j + nb)),
                  pl.BlockSpec((bm, bn), lambda i, j: (i, j)),
                  pl.BlockSpec((bm, bn), lambda i, j: (i, j + nb))],
        out_specs=pl.BlockSpec((bm, bn), lambda i, j: (i, j)),
        out_shape=jax.ShapeDtypeStruct((m, d), BF16),
        compiler_params=_params(("arbitrary", "arbitrary")),
        name=name,
    )(ya, gy, w_up, w_glu, w_glu, gates, gates)


def _softplus(x):
    return jnp.maximum(x, 0.0) + jnp.log1p(jnp.exp(-jnp.abs(x)))


def _ssd_kernel(z_ref, xbc_ref, dt_ref, cw_ref, cb_ref, dtb_ref, alog_ref, dsk_ref, nw_ref,
                y_ref,
                ext_ref, xs_ref, bm_ref, cm_ref, st_ref, ya_ref, acsg_ref, acst_ref, dtt_ref, sel_ref):
    q = SSD_Q
    c = pl.program_id(0)

    @pl.when(c == 0)
    def _():
        ext_ref[0:8, :] = jnp.zeros((8, ext_ref.shape[1]), F32)
        st_ref[...] = jnp.zeros_like(st_ref)
        k_id = lax.broadcasted_iota(jnp.int32, sel_ref.shape, 0)
        j_id = lax.broadcasted_iota(jnp.int32, sel_ref.shape, 1) // LANES
        sel_ref[...] = jnp.where(k_id == j_id, 1.0, 0.0).astype(BF16)

    ext_ref[8:8 + q, :] = xbc_ref[...]
    for blk in range(ext_ref.shape[1] // GW):
        cols = slice(blk * GW, (blk + 1) * GW)
        acc = cb_ref[:, cols] + cw_ref[3:4, cols] * ext_ref[8:8 + q, cols]
        for k in range(SSD_CONV - 1):
            acc = acc + cw_ref[k:k + 1, cols] * ext_ref[5 + k:5 + k + q, cols]
        act = acc * jax.nn.sigmoid(acc)
        if blk < SSD_G:
            xs_ref[blk] = act
        else:
            for i in range(GW // SSD_N):
                gi = (blk - SSD_G) * (GW // SSD_N) + i
                piece = act[:, i * SSD_N:(i + 1) * SSD_N]
                if gi < SSD_G:
                    bm_ref[gi] = piece
                else:
                    cm_ref[gi - SSD_G] = piece
    ext_ref[0:8, :] = ext_ref[q:q + 8, :]

    dt = _softplus(dt_ref[...] + dtb_ref[...])
    acs = dt * (-jnp.exp(alog_ref[...]))
    rows = lax.broadcasted_iota(jnp.int32, (q, LANES), 0)
    lanes = lax.broadcasted_iota(jnp.int32, (q, LANES), 1)
    sh = 1
    while sh < q:
        acs = acs + jnp.where(rows >= sh, pltpu.roll(acs, sh, axis=0), 0.0)
        sh *= 2
    acst_ref[...] = acs.T
    dtt_ref[...] = dt.T
    for g in range(SSD_G):
        acsg_ref[g] = acs if g == 0 else pltpu.roll(acs, LANES - SSD_J * g, axis=1)
    causal = rows >= lanes

    left = lanes < SSD_P

    def group_body(g, carry):
        b_g = bm_ref[g]
        c_g = cm_ref[g]
        cb = lax.dot_general(c_g.astype(BF16), b_g.astype(BF16), (((1,), (1,)), ((), ())),
                             preferred_element_type=F32)
        b_t = b_g.T
        a_blk = acsg_ref[g]
        a_hi = a_blk.astype(BF16)
        a_r1 = a_blk - a_hi.astype(F32)
        a_mid = a_r1.astype(BF16)
        a_lo = (a_r1 - a_mid.astype(F32)).astype(BF16)
        parts = jnp.dot(jnp.concatenate([a_hi, a_mid, a_lo], axis=0), sel_ref[...],
                        preferred_element_type=F32)
        colb = parts[0:q] + parts[q:2 * q] + parts[2 * q:3 * q]
        g8 = pl.multiple_of(g * SSD_J, SSD_J)
        a_t = acst_ref[pl.ds(g8, SSD_J), :]
        d_t = dtt_ref[pl.ds(g8, SSD_J), :]
        for pr in range(SSD_J // 2):
            pcols = slice(pr * LANES, (pr + 1) * LANES)
            x_p = xs_ref[g, :, pcols]
            s_p = st_ref[g, :, pcols]
            lhs, bw, ends = [], [], []
            for j in (2 * pr, 2 * pr + 1):
                col = colb[:, j * LANES:(j + 1) * LANES]
                row = a_t[j:j + 1, :]
                drow = d_t[j:j + 1, :]
                lhs.append(jnp.where(causal, jnp.exp(col - row), 0.0) * (cb * drow))
                lhs.append(jnp.exp(col) * c_g)
                last = row[:, q - 1:q]
                bw.append(b_t * (drow * jnp.exp(last - row)))
                ends.append(jnp.exp(last))
            x_l = jnp.where(left, x_p, 0.0)
            x_r = x_p - x_l
            s_l = jnp.where(left, s_p, 0.0)
            s_r = s_p - s_l
            lhs = jnp.concatenate(lhs, axis=1).astype(BF16)
            rhs = jnp.concatenate([x_l, s_l, x_r, s_r], axis=0).astype(BF16)
            ya_ref[g, :, pcols] = jnp.dot(lhs, rhs, preferred_element_type=F32)
            bw = jnp.concatenate(bw, axis=1).astype(BF16)
            xd = jnp.concatenate([x_l, x_r], axis=0).astype(BF16)
            decay = jnp.where(left[0:1, :], ends[0], ends[1])
            st_ref[g, :, pcols] = s_p * decay + jnp.dot(bw, xd, preferred_element_type=F32)
        return carry

    lax.fori_loop(0, SSD_G, group_body, 0)

    for g in range(SSD_G):
        cols = slice(g * GW, (g + 1) * GW)
        y = ya_ref[g] + xs_ref[g] * dsk_ref[:, cols]
        zz = z_ref[:, cols]
        y = y * (zz * jax.nn.sigmoid(zz))
        ms = jnp.mean(y * y, axis=-1, keepdims=True)
        y_ref[:, cols] = (y * lax.rsqrt(ms + EPS) * nw_ref[:, cols]).astype(y_ref.dtype)


def _ssd(z, xbc, dt, conv_w, conv_b, dt_bias, a_log, d_ssd, ssd_norm, name):
    m, inner = z.shape
    cdim = xbc.shape[1]
    heads = dt_bias.shape[0]
    q = SSD_Q
    pad = LANES - heads
    dtb = jnp.pad(dt_bias, (0, pad)).reshape(1, LANES)
    alog = jnp.pad(a_log, (0, pad)).reshape(1, LANES)
    dsk = jnp.repeat(d_ssd, SSD_P).reshape(1, inner)
    full = lambda r, w: pl.BlockSpec((r, w), lambda c: (0, 0))
    return pl.pallas_call(
        _ssd_kernel,
        grid=(m // q,),
        in_specs=[pl.BlockSpec((q, inner), lambda c: (c, 0)),
                  pl.BlockSpec((q, cdim), lambda c: (c, 0)),
                  pl.BlockSpec((q, LANES), lambda c: (c, 0)),
                  full(SSD_CONV, cdim), full(1, cdim), full(1, LANES), full(1, LANES),
                  full(1, inner), full(1, inner)],
        out_specs=pl.BlockSpec((q, inner), lambda c: (c, 0)),
        out_shape=jax.ShapeDtypeStruct((m, inner), BF16),
        scratch_shapes=[pltpu.VMEM((q + 8, cdim), F32),
                        pltpu.VMEM((SSD_G, q, GW), F32),
                        pltpu.VMEM((SSD_G, q, SSD_N), F32),
                        pltpu.VMEM((SSD_G, q, SSD_N), F32),
                        pltpu.VMEM((SSD_G, SSD_N, GW), F32),
                        pltpu.VMEM((SSD_G, q, GW), F32),
                        pltpu.VMEM((SSD_G, q, LANES), F32),
                        pltpu.VMEM((LANES, q), F32),
                        pltpu.VMEM((LANES, q), F32),
                        pltpu.VMEM((LANES, SSD_J * LANES), BF16)],
        compiler_params=_params(("arbitrary",)),
        name=name,
    )(z, xbc, dt, conv_w, conv_b.reshape(1, cdim), dtb, alog, dsk, ssd_norm.reshape(1, inner))


def _gelu_tanh(x):
    return 0.5 * x * (1.0 + jnp.tanh(math.sqrt(2.0 / math.pi) * (x + 0.044715 * (x * x * x))))


def _s5_kernel(u_ref, kexp_ref, wdup_ref, vtdup_ref, lq_ref, dsk_ref, o_ref,
               t_ref, w_ref, vt_ref, ucat_ref, s_ref, h_ref, yscr_ref):
    nc = ucat_ref.shape[0]
    half = s_ref.shape[1] // 2

    @pl.when((pl.program_id(0) == 0) & (pl.program_id(1) == 0))
    def _():
        for s in range(1, S5_Q):
            t_ref[s * LANES:(s + 1) * LANES, 0:s * LANES] = jnp.zeros((LANES, s * LANES), BF16)

    @pl.when(pl.program_id(1) == 0)
    def _():
        h_ref[...] = jnp.zeros_like(h_ref)
        for s in range(S5_Q):
            for t in range(s, S5_Q):
                t_ref[s * LANES:(s + 1) * LANES, t * LANES:(t + 1) * LANES] = kexp_ref[t - s]
        kq = wdup_ref.shape[0]
        grp = (lax.broadcasted_iota(jnp.int32, (kq, LANES), 0) // S5_C) % S5_GB
        lane_half = lax.broadcasted_iota(jnp.int32, (kq, LANES), 1) // S5_P
        per_tile = LANES // S5_P
        tiles = S5_GB // per_tile
        for k in range(2 * tiles):
            ri = k // tiles
            keep = grp == (k % tiles) * per_tile + lane_half
            src = slice(ri * LANES, (ri + 1) * LANES)
            dst = slice(k * LANES, (k + 1) * LANES)
            w_ref[:, dst] = jnp.where(keep, wdup_ref[:, src], 0.0).astype(BF16)
            vt_ref[:, dst] = jnp.where(keep, vtdup_ref[:, src], 0.0).astype(BF16)

    for s in range(S5_Q):
        ucat_ref[:, s * LANES:(s + 1) * LANES] = u_ref[pl.ds(s, nc, stride=S5_Q), :].astype(BF16)
    ucat = ucat_ref[...]
    s_ref[...] = jnp.dot(ucat, w_ref[...], preferred_element_type=F32)
    l_re = lq_ref[:, :half]
    l_im = lq_ref[:, half:]

    def step(c, carry):
        h_re, h_im = carry
        srow = s_ref[pl.ds(c, 1), :]
        s_ref[pl.ds(c, 1), :] = jnp.concatenate([h_re, h_im], axis=1)
        n_re = l_re * h_re - l_im * h_im + srow[:, :half]
        n_im = l_re * h_im + l_im * h_re + srow[:, half:]
        return n_re, n_im

    h_re, h_im = lax.fori_loop(0, nc, step, (h_ref[0:1, :half], h_ref[0:1, half:]))
    h_ref[0:1, :half] = h_re
    h_ref[0:1, half:] = h_im

    y = (jnp.dot(ucat, t_ref[...], preferred_element_type=F32)
         + lax.dot_general(s_ref[...].astype(BF16), vt_ref[...], (((1,), (1,)), ((), ())),
                           preferred_element_type=F32))
    for s in range(S5_Q):
        us = u_ref[pl.ds(s, nc, stride=S5_Q), :]
        yb = y[:, s * LANES:(s + 1) * LANES] + dsk_ref[...] * us
        yscr_ref[pl.ds(s, nc, stride=S5_Q), :] = _gelu_tanh(yb)
    o_ref[...] = yscr_ref[...].astype(o_ref.dtype)


def _s5_matrices(lam_re, lam_im, log_step, b_re, b_im, c_re, c_im):
    hp = lax.Precision.HIGHEST
    g = lam_re.shape[0]
    nb = g // S5_GB
    qn = S5_Q
    step = jnp.exp(log_step)[:, None]
    mag = jnp.exp(lam_re * step)
    ang = lam_im * step
    lb_re, lb_im = mag * jnp.cos(ang), mag * jnp.sin(ang)
    denom = lam_re * lam_re + lam_im * lam_im
    nr, ni = lb_re - 1.0, lb_im
    f_re = (nr * lam_re + ni * lam_im) / denom
    f_im = (ni * lam_re - nr * lam_im) / denom
    bb_re = f_re[..., None] * b_re - f_im[..., None] * b_im
    bb_im = f_re[..., None] * b_im + f_im[..., None] * b_re
    pw_re = [jnp.ones_like(lb_re)]
    pw_im = [jnp.zeros_like(lb_im)]
    for _ in range(qn):
        r, i = pw_re[-1], pw_im[-1]
        pw_re.append(r * lb_re - i * lb_im)
        pw_im.append(r * lb_im + i * lb_re)
    pw_re = jnp.stack(pw_re)
    pw_im = jnp.stack(pw_im)
    cl_re = c_re[None] * pw_re[:qn, :, None, :] - c_im[None] * pw_im[:qn, :, None, :]
    cl_im = c_re[None] * pw_im[:qn, :, None, :] + c_im[None] * pw_re[:qn, :, None, :]
    kt = (jnp.einsum("tgop,gpc->tgco", cl_re, bb_re, precision=hp)
          - jnp.einsum("tgop,gpc->tgco", cl_im, bb_im, precision=hp))
    kt = jnp.tile(kt.reshape(qn, nb, LANES, S5_C), (1, 1, 1, S5_GB))
    ids = jnp.arange(LANES) // S5_C
    kexp = jnp.where(ids[:, None] == ids[None, :], kt, 0.0).transpose(1, 0, 2, 3)

    def per_block(re, im):
        def rows(v):
            v = v.reshape(qn, nb, S5_GB, S5_C, S5_P).transpose(1, 0, 2, 3, 4)
            return v.reshape(nb, qn * LANES, S5_P)
        re, im = rows(re), rows(im)
        return jnp.concatenate([re, re, im, im], axis=-1)

    bt_re = bb_re.transpose(0, 2, 1)[None]
    bt_im = bb_im.transpose(0, 2, 1)[None]
    ps_re = pw_re[:qn][::-1][:, :, None, :]
    ps_im = pw_im[:qn][::-1][:, :, None, :]
    wdup = per_block(ps_re * bt_re - ps_im * bt_im, ps_re * bt_im + ps_im * bt_re)
    m_re = c_re[None] * pw_re[1:, :, None, :] - c_im[None] * pw_im[1:, :, None, :]
    m_im = c_re[None] * pw_im[1:, :, None, :] + c_im[None] * pw_re[1:, :, None, :]
    vtdup = per_block(m_re, -m_im)
    lq = jnp.concatenate([pw_re[qn].reshape(nb, 1, S5_GB * S5_P),
                          pw_im[qn].reshape(nb, 1, S5_GB * S5_P)], axis=-1)
    return kexp.astype(BF16), wdup, vtdup, lq


def _s5(u, mats, d_s5, name):
    m, width = u.shape
    kexp, wdup, vtdup, lq = mats
    nb = width // LANES
    rows = m // S5_ROW_BLOCKS
    nc = rows // S5_Q
    kq = S5_Q * LANES
    ks = lq.shape[2]
    dup = wdup.shape[2]
    return pl.pallas_call(
        _s5_kernel,
        grid=(nb, S5_ROW_BLOCKS),
        in_specs=[pl.BlockSpec((rows, LANES), lambda b, r: (r, b)),
                  pl.BlockSpec((None, S5_Q, LANES, LANES), lambda b, r: (b, 0, 0, 0)),
                  pl.BlockSpec((None, kq, dup), lambda b, r: (b, 0, 0)),
                  pl.BlockSpec((None, kq, dup), lambda b, r: (b, 0, 0)),
                  pl.BlockSpec((None, 1, ks), lambda b, r: (b, 0, 0)),
                  pl.BlockSpec((1, LANES), lambda b, r: (0, b))],
        out_specs=pl.BlockSpec((rows, LANES), lambda b, r: (r, b)),
        out_shape=jax.ShapeDtypeStruct((m, width), BF16),
        scratch_shapes=[pltpu.VMEM((kq, kq), BF16),
                        pltpu.VMEM((kq, ks), BF16),
                        pltpu.VMEM((kq, ks), BF16),
                        pltpu.VMEM((nc, kq), BF16),
                        pltpu.VMEM((nc, ks), F32),
                        pltpu.VMEM((8, ks), F32),
                        pltpu.VMEM((rows, LANES), F32)],
        compiler_params=_params(("arbitrary", "arbitrary")),
        name=name,
    )(u, kexp, wdup, vtdup, lq, d_s5.reshape(1, width))


def kernel(x, meta_tokens, norm_mix, w_in, conv_w, conv_b, dt_bias, a_log, d_ssd, ssd_norm,
           w_ssd_up, lam_re, lam_im, log_step, b_re, b_im, c_re, c_im, d_s5, w_glu, w_out,
           norm_ffn, w_gate_up, w_down, norm_final):
    batch, seq, d = x.shape
    depth = w_in.shape[0]
    heads = dt_bias.shape[1]
    inner = heads * SSD_P
    cdim = conv_w.shape[2]
    s5w = d_s5.shape[1]
    off_xbc, off_dt = inner, inner + cdim
    off_u = off_dt + heads
    off_g = off_u + s5w
    length = N_META + seq
    lp = -(-length // ROW_ALIGN) * ROW_ALIGN

    outs = []
    for b in range(batch):
        h = jnp.concatenate([meta_tokens.astype(F32), x[b], jnp.zeros((lp - length, d), F32)], axis=0)
        for l in range(depth):
            wl = w_in[l]
            w_dt = jnp.pad(wl[:, off_dt:off_u], ((0, 0), (0, LANES - heads))).astype(BF16)
            w_u = wl[:, off_u:off_g].astype(BF16)
            w_g = wl[:, off_g:].astype(BF16)

            hn = _rmsnorm(h, norm_mix[l], BF16, f"norm_mix{l}")
            z = _mm(hn, w_in, F32, f"proj_z{l}", layer=l, col0=0, n=inner)
            xbc = _mm(hn, w_in, F32, f"proj_xbc{l}", layer=l, col0=off_xbc, n=cdim)
            dt = _mm(hn, w_dt, F32, f"proj_dt{l}")
            u = _mm(hn, w_u, F32, f"proj_u{l}")
            gates = _mm(hn, w_g, F32, f"proj_gates{l}", act="sigmoid")

            y_a = _ssd(z, xbc, dt, conv_w[l], conv_b[l], dt_bias[l], a_log[l], d_ssd[l], ssd_norm[l],
                       f"ssd{l}")
            mats = _s5_matrices(lam_re[l], lam_im[l], log_step[l], b_re[l], b_im[l], c_re[l], c_im[l])
            gy = _s5(u, mats, d_s5[l], f"s5{l}")

            merged = _merge(y_a, gy, w_ssd_up[l].astype(BF16), w_glu[l].astype(BF16), gates, f"merge{l}")
            h = _mm_res(merged, w_out, h, f"out_proj{l}", 1376, 512, layer=l)

            hn = _rmsnorm(h, norm_ffn[l], BF16, f"norm_ffn{l}")
            hidden = _swiglu(hn, w_gate_up, l, f"swiglu{l}")
            h = _mm_res(hidden, w_down[l].astype(BF16), h, f"down_proj{l}", 688, 256)
        out = _rmsnorm(h, norm_final, x.dtype, "norm_final")
        outs.append(out[N_META:length])
    return jnp.stack(outs, axis=0)
```

```python
import functools
import math

import jax
import jax.numpy as jnp
from jax import lax
from jax.experimental import pallas as pl
from jax.experimental.pallas import tpu as pltpu

F32 = jnp.float32
BF16 = jnp.bfloat16

N_META = 16
EPS = 1e-6
SSD_P = 64
SSD_N = 128
SSD_G = 8
SSD_J = 8
SSD_Q = 128
SSD_CONV = 4
GW = SSD_J * SSD_P
S5_C = 16
S5_P = 64
S5_Q = 16
S5_GB = 8
S5_ROW_BLOCKS = 3
LANES = 128
ROW_ALIGN = S5_Q * 8 * S5_ROW_BLOCKS
VMEM_LIMIT = 56 * 1024 * 1024


def _pick(n, cap, align=16):
    best = None
    for d in range(align, min(n, cap) + 1, align):
        if n % d == 0:
            best = d
    assert best is not None, (n, cap)
    return best


def _params(sem):
    return pltpu.CompilerParams(dimension_semantics=sem, vmem_limit_bytes=VMEM_LIMIT)


def _rmsnorm_kernel(x_ref, w_ref, o_ref):
    x = x_ref[...]
    ms = jnp.mean(x * x, axis=-1, keepdims=True)
    o_ref[...] = (x * lax.rsqrt(ms + EPS) * w_ref[...]).astype(o_ref.dtype)


def _rmsnorm(h, w, out_dtype, name):
    m, d = h.shape
    bm = _pick(m, 384)
    return pl.pallas_call(
        _rmsnorm_kernel,
        grid=(m // bm,),
        in_specs=[pl.BlockSpec((bm, d), lambda i: (i, 0)),
                  pl.BlockSpec((1, d), lambda i: (0, 0))],
        out_specs=pl.BlockSpec((bm, d), lambda i: (i, 0)),
        out_shape=jax.ShapeDtypeStruct((m, d), out_dtype),
        compiler_params=_params(("arbitrary",)),
        name=name,
    )(h, w.reshape(1, d))


def _mm_kernel(x_ref, w_ref, o_ref, *, act):
    acc = jnp.dot(x_ref[...], w_ref[...].astype(BF16), preferred_element_type=F32)
    if act == "sigmoid":
        acc = jax.nn.sigmoid(acc)
    o_ref[...] = acc.astype(o_ref.dtype)


def _w_spec(w, k, bn, layer, col0):
    if layer is None:
        return pl.BlockSpec((k, bn), lambda i, j: (0, j))
    assert col0 % bn == 0, (col0, bn)
    return pl.BlockSpec((None, k, bn), lambda i, j: (layer, 0, col0 // bn + j))


def _mm(x, w, out_dtype, name, act=None, bm_cap=1376, bn_cap=512, layer=None, col0=0, n=None):
    m, k = x.shape
    n = w.shape[1] if layer is None else n
    bm = _pick(m, bm_cap)
    bn = _pick(n, bn_cap, LANES)
    return pl.pallas_call(
        functools.partial(_mm_kernel, act=act),
        grid=(m // bm, n // bn),
        in_specs=[pl.BlockSpec((bm, k), lambda i, j: (i, 0)),
                  _w_spec(w, k, bn, layer, col0)],
        out_specs=pl.BlockSpec((bm, bn), lambda i, j: (i, j)),
        out_shape=jax.ShapeDtypeStruct((m, n), out_dtype),
        compiler_params=_params(("arbitrary", "arbitrary")),
        name=name,
    )(x, w)


def _conv_silu(acc, tail, cw_ref, cb_ref):
    rows8 = lax.broadcasted_iota(jnp.int32, tail.shape, 0)
    w_last = cw_ref[SSD_CONV - 1:SSD_CONV, :]
    conv = cb_ref[...] + w_last * acc
    top = cb_ref[...] + w_last * acc[0:8]
    for k in range(SSD_CONV - 1):
        sh = SSD_CONV - 1 - k
        w_k = cw_ref[k:k + 1, :]
        shifted = pltpu.roll(acc, sh, axis=0)
        conv = conv + w_k * shifted
        top = top + w_k * jnp.where(rows8 < sh, pltpu.roll(tail, sh, axis=0), shifted[0:8])
    return conv * jax.nn.sigmoid(conv), top * jax.nn.sigmoid(top)


def _mm_conv_kernel(x_ref, w_ref, cw_ref, cb_ref, o_ref, tail_ref):
    i = pl.program_id(0)
    j = pl.program_id(1)
    bm = o_ref.shape[0]

    @pl.when(i == 0)
    def _():
        tail_ref[j] = jnp.zeros(tail_ref.shape[1:], F32)

    acc = jnp.dot(x_ref[...], w_ref[...].astype(BF16), preferred_element_type=F32)
    full, top = _conv_silu(acc, tail_ref[j], cw_ref, cb_ref)
    o_ref[...] = full
    o_ref[0:8, :] = top
    tail_ref[j] = acc[bm - 8:bm]


def _mm_conv(x, w, conv_w, conv_b, name, bm_cap=1376, bn_cap=512):
    m, k = x.shape
    n = w.shape[1]
    bm = _pick(m, bm_cap)
    bn = _pick(n, bn_cap, LANES)
    return pl.pallas_call(
        _mm_conv_kernel,
        grid=(m // bm, n // bn),
        in_specs=[pl.BlockSpec((bm, k), lambda i, j: (i, 0)),
                  pl.BlockSpec((k, bn), lambda i, j: (0, j)),
                  pl.BlockSpec((SSD_CONV, bn), lambda i, j: (0, j)),
                  pl.BlockSpec((1, bn), lambda i, j: (0, j))],
        out_specs=pl.BlockSpec((bm, bn), lambda i, j: (i, j)),
        out_shape=jax.ShapeDtypeStruct((m, n), F32),
        scratch_shapes=[pltpu.VMEM((n // bn, 8, bn), F32)],
        compiler_params=_params(("arbitrary", "arbitrary")),
        name=name,
    )(x, w, conv_w, conv_b.reshape(1, n))


def _mm_res_kernel(x_ref, w_ref, r_ref, o_ref):
    o_ref[...] = r_ref[...] + jnp.dot(x_ref[...], w_ref[...].astype(BF16), preferred_element_type=F32)


def _mm_res(x, w, res, name, bm_cap, bn_cap, layer=None):
    m, k = x.shape
    n = w.shape[-1]
    bm = _pick(m, bm_cap)
    bn = _pick(n, bn_cap, LANES)
    return pl.pallas_call(
        _mm_res_kernel,
        grid=(m // bm, n // bn),
        in_specs=[pl.BlockSpec((bm, k), lambda i, j: (i, 0)),
                  _w_spec(w, k, bn, layer, 0),
                  pl.BlockSpec((bm, bn), lambda i, j: (i, j))],
        out_specs=pl.BlockSpec((bm, bn), lambda i, j: (i, j)),
        out_shape=jax.ShapeDtypeStruct((m, n), F32),
        input_output_aliases={2: 0},
        compiler_params=_params(("arbitrary", "arbitrary")),
        name=name,
    )(x, w, res)


def _swiglu_kernel(x_ref, wg_ref, wu_ref, o_ref):
    x = x_ref[...]
    g = jnp.dot(x, wg_ref[...].astype(BF16), preferred_element_type=F32)
    u = jnp.dot(x, wu_ref[...].astype(BF16), preferred_element_type=F32)
    o_ref[...] = (g * jax.nn.sigmoid(g) * u).astype(o_ref.dtype)


def _swiglu(x, w_gate_up, layer, name):
    m, k = x.shape
    f = w_gate_up.shape[2] // 2
    bm = _pick(m, 1376)
    bn = _pick(f, 512, LANES)
    nb = f // bn
    return pl.pallas_call(
        _swiglu_kernel,
        grid=(m // bm, nb),
        in_specs=[pl.BlockSpec((bm, k), lambda i, j: (i, 0)),
                  pl.BlockSpec((None, k, bn), lambda i, j: (layer, 0, j)),
                  pl.BlockSpec((None, k, bn), lambda i, j: (layer, 0, j + nb))],
        out_specs=pl.BlockSpec((bm, bn), lambda i, j: (i, j)),
        out_shape=jax.ShapeDtypeStruct((m, f), BF16),
        compiler_params=_params(("arbitrary", "arbitrary")),
        name=name,
    )(x, w_gate_up, w_gate_up)


def _merge_kernel(ya_ref, gy_ref, wup_ref, wg1_ref, wg2_ref, ga_ref, gb_ref, o_ref):
    a = jnp.dot(ya_ref[...], wup_ref[...], preferred_element_type=F32)
    gy = gy_ref[...]
    g1 = jnp.dot(gy, wg1_ref[...], preferred_element_type=F32)
    g2 = jnp.dot(gy, wg2_ref[...], preferred_element_type=F32)
    o_ref[...] = (ga_ref[...] * a + gb_ref[...] * (g1 * jax.nn.sigmoid(g2))).astype(o_ref.dtype)


def _merge(ya, gy, w_up, w_glu, gates, name):
    m, d = ya.shape
    k2 = gy.shape[1]
    bm = _pick(m, 688)
    bn = _pick(d, 512, LANES)
    nb = d // bn
    return pl.pallas_call(
        _merge_kernel,
        grid=(m // bm, nb),
        in_specs=[pl.BlockSpec((bm, d), lambda i, j: (i, 0)),
                  pl.BlockSpec((bm, k2), lambda i, j: (i, 0)),
                  pl.BlockSpec((d, bn), lambda i, j: (0, j)),
                  pl.BlockSpec((k2, bn), lambda i, j: (0, j)),
                  pl.BlockSpec((k2, bn), lambda i, j: (0, j + nb)),
                  pl.BlockSpec((bm, bn), lambda i, j: (i, j)),
                  pl.BlockSpec((bm, bn), lambda i, j: (i, j + nb))],
        out_specs=pl.BlockSpec((bm, bn), lambda i, j: (i, j)),
        out_shape=jax.ShapeDtypeStruct((m, d), BF16),
        compiler_params=_params(("arbitrary", "arbitrary")),
        name=name,
    )(ya, gy, w_up, w_glu, w_glu, gates, gates)


def _softplus(x):
    return jnp.maximum(x, 0.0) + jnp.log1p(jnp.exp(-jnp.abs(x)))


def _ssd_kernel(z_ref, xbc_ref, dt_ref, dtb_ref, alog_ref, dsk_ref, nw_ref,
                y_ref,
                xs_ref, bm_ref, cm_ref, st_ref, ya_ref, acsg_ref, acst_ref, dtt_ref, sel_ref):
    q = SSD_Q
    c = pl.program_id(0)

    @pl.when(c == 0)
    def _():
        st_ref[...] = jnp.zeros_like(st_ref)
        k_id = lax.broadcasted_iota(jnp.int32, sel_ref.shape, 0)
        j_id = lax.broadcasted_iota(jnp.int32, sel_ref.shape, 1) // LANES
        sel_ref[...] = jnp.where(k_id == j_id, 1.0, 0.0).astype(BF16)

    n_x = SSD_G * GW
    for g in range(SSD_G):
        xs_ref[g] = xbc_ref[:, g * GW:(g + 1) * GW]
        bm_ref[g] = xbc_ref[:, n_x + g * SSD_N:n_x + (g + 1) * SSD_N]
        cm_ref[g] = xbc_ref[:, n_x + (SSD_G + g) * SSD_N:n_x + (SSD_G + g + 1) * SSD_N]

    dt = _softplus(dt_ref[...] + dtb_ref[...])
    acs = dt * (-jnp.exp(alog_ref[...]))
    rows = lax.broadcasted_iota(jnp.int32, (q, LANES), 0)
    lanes = lax.broadcasted_iota(jnp.int32, (q, LANES), 1)
    sh = 1
    while sh < q:
        acs = acs + jnp.where(rows >= sh, pltpu.roll(acs, sh, axis=0), 0.0)
        sh *= 2
    acst_ref[...] = acs.T
    dtt_ref[...] = dt.T
    for g in range(SSD_G):
        acsg_ref[g] = acs if g == 0 else pltpu.roll(acs, LANES - SSD_J * g, axis=1)
    causal = rows >= lanes

    left = lanes < SSD_P

    def group_body(g, carry):
        b_g = bm_ref[g]
        c_g = cm_ref[g]
        cb = lax.dot_general(c_g.astype(BF16), b_g.astype(BF16), (((1,), (1,)), ((), ())),
                             preferred_element_type=F32)
        b_t = b_g.T
        a_blk = acsg_ref[g]
        a_hi = a_blk.astype(BF16)
        a_r1 = a_blk - a_hi.astype(F32)
        a_mid = a_r1.astype(BF16)
        a_lo = (a_r1 - a_mid.astype(F32)).astype(BF16)
        parts = jnp.dot(jnp.concatenate([a_hi, a_mid, a_lo], axis=0), sel_ref[...],
                        preferred_element_type=F32)
        colb = parts[0:q] + parts[q:2 * q] + parts[2 * q:3 * q]
        g8 = pl.multiple_of(g * SSD_J, SSD_J)
        a_t = acst_ref[pl.ds(g8, SSD_J), :]
        d_t = dtt_ref[pl.ds(g8, SSD_J), :]
        for pr in range(SSD_J // 2):
            pcols = slice(pr * LANES, (pr + 1) * LANES)
            x_p = xs_ref[g, :, pcols]
            s_p = st_ref[g, :, pcols]
            lhs, bw, ends = [], [], []
            for j in (2 * pr, 2 * pr + 1):
                col = colb[:, j * LANES:(j + 1) * LANES]
                row = a_t[j:j + 1, :]
                drow = d_t[j:j + 1, :]
                lhs.append(jnp.where(causal, jnp.exp(col - row), 0.0) * (cb * drow))
                lhs.append(jnp.exp(col) * c_g)
                last = row[:, q - 1:q]
                bw.append(b_t * (drow * jnp.exp(last - row)))
                ends.append(jnp.exp(last))
            x_l = jnp.where(left, x_p, 0.0)
            x_r = x_p - x_l
            s_l = jnp.where(left, s_p, 0.0)
            s_r = s_p - s_l
            lhs = jnp.concatenate(lhs, axis=1).astype(BF16)
            rhs = jnp.concatenate([x_l, s_l, x_r, s_r], axis=0).astype(BF16)
            ya_ref[g, :, pcols] = jnp.dot(lhs, rhs, preferred_element_type=F32)
            bw = jnp.concatenate(bw, axis=1).astype(BF16)
            xd = jnp.concatenate([x_l, x_r], axis=0).astype(BF16)
            decay = jnp.where(left[0:1, :], ends[0], ends[1])
            st_ref[g, :, pcols] = s_p * decay + jnp.dot(bw, xd, preferred_element_type=F32)
        return carry

    lax.fori_loop(0, SSD_G, group_body, 0)

    for g in range(SSD_G):
        cols = slice(g * GW, (g + 1) * GW)
        y = ya_ref[g] + xs_ref[g] * dsk_ref[:, cols]
        zz = z_ref[:, cols]
        y = y * (zz * jax.nn.sigmoid(zz))
        ms = jnp.mean(y * y, axis=-1, keepdims=True)
        y_ref[:, cols] = (y * lax.rsqrt(ms + EPS) * nw_ref[:, cols]).astype(y_ref.dtype)


def _ssd(z, xbc, dt, dt_bias, a_log, d_ssd, ssd_norm, name):
    m, inner = z.shape
    cdim = xbc.shape[1]
    heads = dt_bias.shape[0]
    q = SSD_Q
    pad = LANES - heads
    dtb = jnp.pad(dt_bias, (0, pad)).reshape(1, LANES)
    alog = jnp.pad(a_log, (0, pad)).reshape(1, LANES)
    dsk = jnp.repeat(d_ssd, SSD_P).reshape(1, inner)
    full = lambda r, w: pl.BlockSpec((r, w), lambda c: (0, 0))
    return pl.pallas_call(
        _ssd_kernel,
        grid=(m // q,),
        in_specs=[pl.BlockSpec((q, inner), lambda c: (c, 0)),
                  pl.BlockSpec((q, cdim), lambda c: (c, 0)),
                  pl.BlockSpec((q, LANES), lambda c: (c, 0)),
                  full(1, LANES), full(1, LANES), full(1, inner), full(1, inner)],
        out_specs=pl.BlockSpec((q, inner), lambda c: (c, 0)),
        out_shape=jax.ShapeDtypeStruct((m, inner), BF16),
        scratch_shapes=[pltpu.VMEM((SSD_G, q, GW), F32),
                        pltpu.VMEM((SSD_G, q, SSD_N), F32),
                        pltpu.VMEM((SSD_G, q, SSD_N), F32),
                        pltpu.VMEM((SSD_G, SSD_N, GW), F32),
                        pltpu.VMEM((SSD_G, q, GW), F32),
                        pltpu.VMEM((SSD_G, q, LANES), F32),
                        pltpu.VMEM((LANES, q), F32),
                        pltpu.VMEM((LANES, q), F32),
                        pltpu.VMEM((LANES, SSD_J * LANES), BF16)],
        compiler_params=_params(("arbitrary",)),
        name=name,
    )(z, xbc, dt, dtb, alog, dsk, ssd_norm.reshape(1, inner))


def _gelu_tanh(x):
    return 0.5 * x * (1.0 + jnp.tanh(math.sqrt(2.0 / math.pi) * (x + 0.044715 * (x * x * x))))


def _s5_kernel(u_ref, wdup_ref, vtdup_ref, lq_ref, dsk_ref, o_ref,
               t_ref, w_ref, vt_ref, ucat_ref, s_ref, h_ref, yscr_ref):
    nc = ucat_ref.shape[0]
    kq = t_ref.shape[0]
    half = s_ref.shape[1] // 2

    @pl.when((pl.program_id(0) == 0) & (pl.program_id(1) == 0))
    def _():
        for s in range(1, S5_Q):
            t_ref[s * LANES:(s + 1) * LANES, 0:s * LANES] = jnp.zeros((LANES, s * LANES), BF16)

    @pl.when(pl.program_id(1) == 0)
    def _():
        h_ref[...] = jnp.zeros_like(h_ref)
        kv = vtdup_ref.shape[0]
        grp = (lax.broadcasted_iota(jnp.int32, (kv, LANES), 0) // S5_C) % S5_GB
        lane_half = lax.broadcasted_iota(jnp.int32, (kv, LANES), 1) // S5_P
        per_tile = LANES // S5_P
        tiles = S5_GB // per_tile
        for k in range(2 * tiles):
            ri = k // tiles
            keep = grp == (k % tiles) * per_tile + lane_half
            src = slice(ri * LANES, (ri + 1) * LANES)
            dst = slice(k * LANES, (k + 1) * LANES)
            w_ref[:, dst] = jnp.where(keep[:kq], wdup_ref[:, src], 0.0).astype(BF16)
            vt_ref[:, dst] = jnp.where(keep, vtdup_ref[:, src], 0.0).astype(BF16)
        k_all = lax.dot_general(w_ref[kq - LANES:kq, :], vt_ref[0:kq, :], (((1,), (1,)), ((), ())),
                                preferred_element_type=F32).astype(BF16)
        for s in range(S5_Q):
            for t in range(s, S5_Q):
                t_ref[s * LANES:(s + 1) * LANES, t * LANES:(t + 1) * LANES] = (
                    k_all[:, (t - s) * LANES:(t - s + 1) * LANES])

    for s in range(S5_Q):
        ucat_ref[:, s * LANES:(s + 1) * LANES] = u_ref[pl.ds(s, nc, stride=S5_Q), :].astype(BF16)
    ucat = ucat_ref[...]
    s_ref[...] = jnp.dot(ucat, w_ref[...], preferred_element_type=F32)
    l_re = lq_ref[:, :half]
    l_im = lq_ref[:, half:]

    def step(c, carry):
        h_re, h_im = carry
        srow = s_ref[pl.ds(c, 1), :]
        s_ref[pl.ds(c, 1), :] = jnp.concatenate([h_re, h_im], axis=1)
        n_re = l_re * h_re - l_im * h_im + srow[:, :half]
        n_im = l_re * h_im + l_im * h_re + srow[:, half:]
        return n_re, n_im

    h_re, h_im = lax.fori_loop(0, nc, step, (h_ref[0:1, :half], h_ref[0:1, half:]))
    h_ref[0:1, :half] = h_re
    h_ref[0:1, half:] = h_im

    y = (jnp.dot(ucat, t_ref[...], preferred_element_type=F32)
         + lax.dot_general(s_ref[...].astype(BF16), vt_ref[LANES:, :], (((1,), (1,)), ((), ())),
                           preferred_element_type=F32))
    for s in range(S5_Q):
        us = u_ref[pl.ds(s, nc, stride=S5_Q), :]
        yb = y[:, s * LANES:(s + 1) * LANES] + dsk_ref[...] * us
        yscr_ref[pl.ds(s, nc, stride=S5_Q), :] = _gelu_tanh(yb)
    o_ref[...] = yscr_ref[...].astype(o_ref.dtype)


def _s5_matrices(lam_re, lam_im, log_step, b_re, b_im, c_re, c_im):
    g = lam_re.shape[0]
    nb = g // S5_GB
    qn = S5_Q
    step = jnp.exp(log_step)[:, None]
    mag = jnp.exp(lam_re * step)
    ang = lam_im * step
    lb_re, lb_im = mag * jnp.cos(ang), mag * jnp.sin(ang)
    denom = lam_re * lam_re + lam_im * lam_im
    nr, ni = lb_re - 1.0, lb_im
    f_re = (nr * lam_re + ni * lam_im) / denom
    f_im = (ni * lam_re - nr * lam_im) / denom
    bb_re = f_re[..., None] * b_re - f_im[..., None] * b_im
    bb_im = f_re[..., None] * b_im + f_im[..., None] * b_re
    pw_re = [jnp.ones_like(lb_re)]
    pw_im = [jnp.zeros_like(lb_im)]
    for _ in range(qn):
        r, i = pw_re[-1], pw_im[-1]
        pw_re.append(r * lb_re - i * lb_im)
        pw_im.append(r * lb_im + i * lb_re)
    pw_re = jnp.stack(pw_re)
    pw_im = jnp.stack(pw_im)
    def per_block(re, im):
        def rows(v):
            nq = v.shape[0]
            v = v.reshape(nq, nb, S5_GB, S5_C, S5_P).transpose(1, 0, 2, 3, 4)
            return v.reshape(nb, nq * LANES, S5_P)
        re, im = rows(re), rows(im)
        return jnp.concatenate([re, re, im, im], axis=-1)

    bt_re = bb_re.transpose(0, 2, 1)[None]
    bt_im = bb_im.transpose(0, 2, 1)[None]
    ps_re = pw_re[:qn][::-1][:, :, None, :]
    ps_im = pw_im[:qn][::-1][:, :, None, :]
    wdup = per_block(ps_re * bt_re - ps_im * bt_im, ps_re * bt_im + ps_im * bt_re)
    m_re = c_re[None] * pw_re[:, :, None, :] - c_im[None] * pw_im[:, :, None, :]
    m_im = c_re[None] * pw_im[:, :, None, :] + c_im[None] * pw_re[:, :, None, :]
    vtdup = per_block(m_re, -m_im)
    lq = jnp.concatenate([pw_re[qn].reshape(nb, 1, S5_GB * S5_P),
                          pw_im[qn].reshape(nb, 1, S5_GB * S5_P)], axis=-1)
    return wdup, vtdup, lq


def _s5(u, mats, d_s5, name):
    m, width = u.shape
    wdup, vtdup, lq = mats
    nb = width // LANES
    rows = m // S5_ROW_BLOCKS
    nc = rows // S5_Q
    kq = S5_Q * LANES
    kv = vtdup.shape[1]
    ks = lq.shape[2]
    dup = wdup.shape[2]
    return pl.pallas_call(
        _s5_kernel,
        grid=(nb, S5_ROW_BLOCKS),
        in_specs=[pl.BlockSpec((rows, LANES), lambda b, r: (r, b)),
                  pl.BlockSpec((None, kq, dup), lambda b, r: (b, 0, 0)),
                  pl.BlockSpec((None, kv, dup), lambda b, r: (b, 0, 0)),
                  pl.BlockSpec((None, 1, ks), lambda b, r: (b, 0, 0)),
                  pl.BlockSpec((1, LANES), lambda b, r: (0, b))],
        out_specs=pl.BlockSpec((rows, LANES), lambda b, r: (r, b)),
        out_shape=jax.ShapeDtypeStruct((m, width), BF16),
        scratch_shapes=[pltpu.VMEM((kq, kq), BF16),
                        pltpu.VMEM((kq, ks), BF16),
                        pltpu.VMEM((kv, ks), BF16),
                        pltpu.VMEM((nc, kq), BF16),
                        pltpu.VMEM((nc, ks), F32),
                        pltpu.VMEM((8, ks), F32),
                        pltpu.VMEM((rows, LANES), F32)],
        compiler_params=_params(("arbitrary", "arbitrary")),
        name=name,
    )(u, wdup, vtdup, lq, d_s5.reshape(1, width))


def kernel(x, meta_tokens, norm_mix, w_in, conv_w, conv_b, dt_bias, a_log, d_ssd, ssd_norm,
           w_ssd_up, lam_re, lam_im, log_step, b_re, b_im, c_re, c_im, d_s5, w_glu, w_out,
           norm_ffn, w_gate_up, w_down, norm_final):
    batch, seq, d = x.shape
    depth = w_in.shape[0]
    heads = dt_bias.shape[1]
    inner = heads * SSD_P
    cdim = conv_w.shape[2]
    s5w = d_s5.shape[1]
    off_xbc, off_dt = inner, inner + cdim
    off_u = off_dt + heads
    off_g = off_u + s5w
    length = N_META + seq
    lp = -(-length // ROW_ALIGN) * ROW_ALIGN

    outs = []
    for b in range(batch):
        h = jnp.concatenate([meta_tokens.astype(F32), x[b], jnp.zeros((lp - length, d), F32)], axis=0)
        for l in range(depth):
            wl = w_in[l]
            w_z = wl[:, :off_xbc].astype(BF16)
            w_xbc = wl[:, off_xbc:off_dt].astype(BF16)
            w_dt = jnp.pad(wl[:, off_dt:off_u], ((0, 0), (0, LANES - heads))).astype(BF16)
            w_u = wl[:, off_u:off_g].astype(BF16)
            w_g = wl[:, off_g:].astype(BF16)

            hn = _rmsnorm(h, norm_mix[l], BF16, f"norm_mix{l}")
            z = _mm(hn, w_z, F32, f"proj_z{l}")
            xbc = _mm_conv(hn, w_xbc, conv_w[l], conv_b[l], f"proj_xbc{l}")
            dt = _mm(hn, w_dt, F32, f"proj_dt{l}")
            u = _mm(hn, w_u, F32, f"proj_u{l}")
            gates = _mm(hn, w_g, F32, f"proj_gates{l}", act="sigmoid")

            y_a = _ssd(z, xbc, dt, dt_bias[l], a_log[l], d_ssd[l], ssd_norm[l], f"ssd{l}")
            mats = _s5_matrices(lam_re[l], lam_im[l], log_step[l], b_re[l], b_im[l], c_re[l], c_im[l])
            gy = _s5(u, mats, d_s5[l], f"s5{l}")

            merged = _merge(y_a, gy, w_ssd_up[l].astype(BF16), w_glu[l].astype(BF16), gates, f"merge{l}")
            h = _mm_res(merged, w_out, h, f"out_proj{l}", 1376, 512, layer=l)

            hn = _rmsnorm(h, norm_ffn[l], BF16, f"norm_ffn{l}")
            hidden = _swiglu(hn, w_gate_up, l, f"swiglu{l}")
            h = _mm_res(hidden, w_down[l].astype(BF16), h, f"down_proj{l}", 688, 256)
        out = _rmsnorm(h, norm_final, x.dtype, "norm_final")
        outs.append(out[N_META:length])
    return jnp.stack(outs, axis=0)
```

```python
import functools
import math

import jax
import jax.numpy as jnp
from jax import lax
from jax.experimental import pallas as pl
from jax.experimental.pallas import tpu as pltpu

F32 = jnp.float32
BF16 = jnp.bfloat16

N_META = 16
EPS = 1e-6
SSD_P = 64
SSD_N = 128
SSD_G = 8
SSD_J = 8
SSD_Q = 128
SSD_CONV = 4
GW = SSD_J * SSD_P
S5_C = 16
S5_P = 64
S5_Q = 16
S5_GB = 8
S5_ROW_BLOCKS = 3
LANES = 128
ROW_ALIGN = S5_Q * 8 * S5_ROW_BLOCKS
VMEM_LIMIT = 56 * 1024 * 1024


def _pick(n, cap, align=16):
    best = None
    for d in range(align, min(n, cap) + 1, align):
        if n % d == 0:
            best = d
    assert best is not None, (n, cap)
    return best


def _params(sem):
    return pltpu.CompilerParams(dimension_semantics=sem, vmem_limit_bytes=VMEM_LIMIT)


def _rmsnorm_kernel(x_ref, w_ref, o_ref):
    x = x_ref[...]
    ms = jnp.mean(x * x, axis=-1, keepdims=True)
    o_ref[...] = (x * lax.rsqrt(ms + EPS) * w_ref[...]).astype(o_ref.dtype)


def _rmsnorm(h, w, out_dtype, name):
    m, d = h.shape
    bm = _pick(m, 384)
    return pl.pallas_call(
        _rmsnorm_kernel,
        grid=(m // bm,),
        in_specs=[pl.BlockSpec((bm, d), lambda i: (i, 0)),
                  pl.BlockSpec((1, d), lambda i: (0, 0))],
        out_specs=pl.BlockSpec((bm, d), lambda i: (i, 0)),
        out_shape=jax.ShapeDtypeStruct((m, d), out_dtype),
        compiler_params=_params(("arbitrary",)),
        name=name,
    )(h, w.reshape(1, d))


def _mm_kernel(x_ref, w_ref, o_ref, *, act):
    acc = jnp.dot(x_ref[...], w_ref[...].astype(BF16), preferred_element_type=F32)
    if act == "sigmoid":
        acc = jax.nn.sigmoid(acc)
    o_ref[...] = acc.astype(o_ref.dtype)


def _w_spec(w, k, bn, layer, col0):
    if layer is None:
        return pl.BlockSpec((k, bn), lambda i, j: (0, j))
    assert col0 % bn == 0, (col0, bn)
    return pl.BlockSpec((None, k, bn), lambda i, j: (layer, 0, col0 // bn + j))


def _mm(x, w, out_dtype, name, act=None, bm_cap=1376, bn_cap=512, layer=None, col0=0, n=None):
    m, k = x.shape
    n = w.shape[1] if layer is None else n
    bm = _pick(m, bm_cap)
    bn = _pick(n, bn_cap, LANES)
    return pl.pallas_call(
        functools.partial(_mm_kernel, act=act),
        grid=(m // bm, n // bn),
        in_specs=[pl.BlockSpec((bm, k), lambda i, j: (i, 0)),
                  _w_spec(w, k, bn, layer, col0)],
        out_specs=pl.BlockSpec((bm, bn), lambda i, j: (i, j)),
        out_shape=jax.ShapeDtypeStruct((m, n), out_dtype),
        compiler_params=_params(("arbitrary", "arbitrary")),
        name=name,
    )(x, w)


def _conv_silu(acc, tail, cw_ref, cb_ref):
    rows8 = lax.broadcasted_iota(jnp.int32, tail.shape, 0)
    w_last = cw_ref[SSD_CONV - 1:SSD_CONV, :]
    conv = cb_ref[...] + w_last * acc
    top = cb_ref[...] + w_last * acc[0:8]
    for k in range(SSD_CONV - 1):
        sh = SSD_CONV - 1 - k
        w_k = cw_ref[k:k + 1, :]
        shifted = pltpu.roll(acc, sh, axis=0)
        conv = conv + w_k * shifted
        top = top + w_k * jnp.where(rows8 < sh, pltpu.roll(tail, sh, axis=0), shifted[0:8])
    return conv * jax.nn.sigmoid(conv), top * jax.nn.sigmoid(top)


def _mm_conv_kernel(x_ref, w_ref, cw_ref, cb_ref, o_ref, tail_ref):
    i = pl.program_id(0)
    j = pl.program_id(1)
    bm = o_ref.shape[0]

    @pl.when(i == 0)
    def _():
        tail_ref[j] = jnp.zeros(tail_ref.shape[1:], F32)

    acc = jnp.dot(x_ref[...], w_ref[...].astype(BF16), preferred_element_type=F32)
    full, top = _conv_silu(acc, tail_ref[j], cw_ref, cb_ref)
    o_ref[...] = full
    o_ref[0:8, :] = top
    tail_ref[j] = acc[bm - 8:bm]


def _mm_conv(x, w, conv_w, conv_b, name, bm_cap=1376, bn_cap=512):
    m, k = x.shape
    n = w.shape[1]
    bm = _pick(m, bm_cap)
    bn = _pick(n, bn_cap, LANES)
    return pl.pallas_call(
        _mm_conv_kernel,
        grid=(m // bm, n // bn),
        in_specs=[pl.BlockSpec((bm, k), lambda i, j: (i, 0)),
                  pl.BlockSpec((k, bn), lambda i, j: (0, j)),
                  pl.BlockSpec((SSD_CONV, bn), lambda i, j: (0, j)),
                  pl.BlockSpec((1, bn), lambda i, j: (0, j))],
        out_specs=pl.BlockSpec((bm, bn), lambda i, j: (i, j)),
        out_shape=jax.ShapeDtypeStruct((m, n), F32),
        scratch_shapes=[pltpu.VMEM((n // bn, 8, bn), F32)],
        compiler_params=_params(("arbitrary", "arbitrary")),
        name=name,
    )(x, w, conv_w, conv_b.reshape(1, n))


def _mm_res_kernel(x_ref, w_ref, r_ref, o_ref):
    o_ref[...] = r_ref[...] + jnp.dot(x_ref[...], w_ref[...].astype(BF16), preferred_element_type=F32)


def _mm_res(x, w, res, name, bm_cap, bn_cap, layer=None):
    m, k = x.shape
    n = w.shape[-1]
    bm = _pick(m, bm_cap)
    bn = _pick(n, bn_cap, LANES)
    return pl.pallas_call(
        _mm_res_kernel,
        grid=(m // bm, n // bn),
        in_specs=[pl.BlockSpec((bm, k), lambda i, j: (i, 0)),
                  _w_spec(w, k, bn, layer, 0),
                  pl.BlockSpec((bm, bn), lambda i, j: (i, j))],
        out_specs=pl.BlockSpec((bm, bn), lambda i, j: (i, j)),
        out_shape=jax.ShapeDtypeStruct((m, n), F32),
        input_output_aliases={2: 0},
        compiler_params=_params(("arbitrary", "arbitrary")),
        name=name,
    )(x, w, res)


def _swiglu_kernel(x_ref, wg_ref, wu_ref, o_ref):
    x = x_ref[...]
    g = jnp.dot(x, wg_ref[...].astype(BF16), preferred_element_type=F32)
    u = jnp.dot(x, wu_ref[...].astype(BF16), preferred_element_type=F32)
    o_ref[...] = (g * jax.nn.sigmoid(g) * u).astype(o_ref.dtype)


def _swiglu(x, w_gate_up, layer, name):
    m, k = x.shape
    f = w_gate_up.shape[2] // 2
    bm = _pick(m, 1376)
    bn = _pick(f, 512, LANES)
    nb = f // bn
    return pl.pallas_call(
        _swiglu_kernel,
        grid=(m // bm, nb),
        in_specs=[pl.BlockSpec((bm, k), lambda i, j: (i, 0)),
                  pl.BlockSpec((None, k, bn), lambda i, j: (layer, 0, j)),
                  pl.BlockSpec((None, k, bn), lambda i, j: (layer, 0, j + nb))],
        out_specs=pl.BlockSpec((bm, bn), lambda i, j: (i, j)),
        out_shape=jax.ShapeDtypeStruct((m, f), BF16),
        compiler_params=_params(("arbitrary", "arbitrary")),
        name=name,
    )(x, w_gate_up, w_gate_up)


def _merge_kernel(ya_ref, gy_ref, wup_ref, wg1_ref, wg2_ref, ga_ref, gb_ref, o_ref):
    a = jnp.dot(ya_ref[...], wup_ref[...], preferred_element_type=F32)
    gy = gy_ref[...]
    g1 = jnp.dot(gy, wg1_ref[...], preferred_element_type=F32)
    g2 = jnp.dot(gy, wg2_ref[...], preferred_element_type=F32)
    o_ref[...] = (ga_ref[...] * a + gb_ref[...] * (g1 * jax.nn.sigmoid(g2))).astype(o_ref.dtype)


def _merge(ya, gy, w_up, w_glu, layer, gates, name):
    m, d = ya.shape
    k2 = gy.shape[1]
    bm = _pick(m, 688)
    bn = _pick(d, 512, LANES)
    nb = d // bn
    return pl.pallas_call(
        _merge_kernel,
        grid=(m // bm, nb),
        in_specs=[pl.BlockSpec((bm, d), lambda i, j: (i, 0)),
                  pl.BlockSpec((bm, k2), lambda i, j: (i, 0)),
                  pl.BlockSpec((None, d, bn), lambda i, j: (layer, 0, j)),
                  pl.BlockSpec((None, k2, bn), lambda i, j: (layer, 0, j)),
                  pl.BlockSpec((None, k2, bn), lambda i, j: (layer, 0, j + nb)),
                  pl.BlockSpec((bm, bn), lambda i, j: (i, j)),
                  pl.BlockSpec((bm, bn), lambda i, j: (i, j + nb))],
        out_specs=pl.BlockSpec((bm, bn), lambda i, j: (i, j)),
        out_shape=jax.ShapeDtypeStruct((m, d), BF16),
        compiler_params=_params(("arbitrary", "arbitrary")),
        name=name,
    )(ya, gy, w_up, w_glu, w_glu, gates, gates)


def _softplus(x):
    return jnp.maximum(x, 0.0) + jnp.log1p(jnp.exp(-jnp.abs(x)))


def _ssd_kernel(z_ref, xbc_ref, dt_ref, dtb_ref, alog_ref, dsk_ref, nw_ref,
                y_ref,
                xs_ref, bm_ref, cm_ref, st_ref, ya_ref, acsg_ref, acst_ref, dtt_ref):
    q = SSD_Q
    c = pl.program_id(0)

    @pl.when(c == 0)
    def _():
        st_ref[...] = jnp.zeros_like(st_ref)

    n_x = SSD_G * GW
    for g in range(SSD_G):
        xs_ref[g] = xbc_ref[:, g * GW:(g + 1) * GW]
        bm_ref[g] = xbc_ref[:, n_x + g * SSD_N:n_x + (g + 1) * SSD_N]
        cm_ref[g] = xbc_ref[:, n_x + (SSD_G + g) * SSD_N:n_x + (SSD_G + g + 1) * SSD_N]

    dt = _softplus(dt_ref[...] + dtb_ref[...])
    acs = dt * (-jnp.exp(alog_ref[...]))
    rows = lax.broadcasted_iota(jnp.int32, (q, LANES), 0)
    lanes = lax.broadcasted_iota(jnp.int32, (q, LANES), 1)
    sh = 1
    while sh < q:
        acs = acs + jnp.where(rows >= sh, pltpu.roll(acs, sh, axis=0), 0.0)
        sh *= 2
    acs = acs * math.log2(math.e)
    acst_ref[...] = acs.T
    dtt_ref[...] = dt.T
    for g in range(SSD_G):
        acsg_ref[g] = acs if g == 0 else pltpu.roll(acs, LANES - SSD_J * g, axis=1)
    causal = rows >= lanes

    left = lanes < SSD_P

    def group_body(g, carry):
        b_g = bm_ref[g]
        c_g = cm_ref[g]
        cb = lax.dot_general(c_g.astype(BF16), b_g.astype(BF16), (((1,), (1,)), ((), ())),
                             preferred_element_type=F32)
        b_t = b_g.T
        a_blk = acsg_ref[g]
        g8 = pl.multiple_of(g * SSD_J, SSD_J)
        a_t = acst_ref[pl.ds(g8, SSD_J), :]
        d_t = dtt_ref[pl.ds(g8, SSD_J), :]
        none = jnp.zeros((SSD_N, SSD_N), F32)
        for pr in range(SSD_J // 2):
            pcols = slice(pr * LANES, (pr + 1) * LANES)
            x_p = xs_ref[g, :, pcols]
            s_p = st_ref[g, :, pcols]
            out_rows, upd_rows, ends = [], [], []
            for j in (2 * pr, 2 * pr + 1):
                col = jnp.broadcast_to(a_blk[:, j:j + 1], (q, LANES))
                row = a_t[j:j + 1, :]
                drow = d_t[j:j + 1, :]
                out_rows.append(jnp.where(causal, jnp.exp2(col - row), 0.0) * (cb * drow))
                out_rows.append(jnp.exp2(col) * c_g)
                last = row[:, q - 1:q]
                upd_rows.append(b_t * (drow * jnp.exp2(last - row)))
                upd_rows.append(none)
                ends.append(jnp.exp2(last))
            x_l = jnp.where(left, x_p, 0.0)
            x_r = x_p - x_l
            s_l = jnp.where(left, s_p, 0.0)
            s_r = s_p - s_l
            lhs = jnp.concatenate([jnp.concatenate(out_rows, axis=1),
                                   jnp.concatenate(upd_rows, axis=1)], axis=0).astype(BF16)
            rhs = jnp.concatenate([x_l, s_l, x_r, s_r], axis=0).astype(BF16)
            res = jnp.dot(lhs, rhs, preferred_element_type=F32)
            ya_ref[g, :, pcols] = res[0:q]
            decay = jnp.where(left[0:1, :], ends[0], ends[1])
            st_ref[g, :, pcols] = s_p * decay + res[q:]
        return carry

    lax.fori_loop(0, SSD_G, group_body, 0, unroll=True)

    for g in range(SSD_G):
        cols = slice(g * GW, (g + 1) * GW)
        y = ya_ref[g] + xs_ref[g] * dsk_ref[:, cols]
        zz = z_ref[:, cols]
        y = y * (zz * jax.nn.sigmoid(zz))
        ms = jnp.mean(y * y, axis=-1, keepdims=True)
        y_ref[:, cols] = (y * lax.rsqrt(ms + EPS) * nw_ref[:, cols]).astype(y_ref.dtype)


def _ssd(z, xbc, dt, dt_bias, a_log, d_ssd, ssd_norm, name):
    m, inner = z.shape
    cdim = xbc.shape[1]
    heads = dt_bias.shape[0]
    q = SSD_Q
    pad = LANES - heads
    dtb = jnp.pad(dt_bias, (0, pad)).reshape(1, LANES)
    alog = jnp.pad(a_log, (0, pad)).reshape(1, LANES)
    dsk = jnp.repeat(d_ssd, SSD_P).reshape(1, inner)
    full = lambda r, w: pl.BlockSpec((r, w), lambda c: (0, 0))
    return pl.pallas_call(
        _ssd_kernel,
        grid=(m // q,),
        in_specs=[pl.BlockSpec((q, inner), lambda c: (c, 0)),
                  pl.BlockSpec((q, cdim), lambda c: (c, 0)),
                  pl.BlockSpec((q, LANES), lambda c: (c, 0)),
                  full(1, LANES), full(1, LANES), full(1, inner), full(1, inner)],
        out_specs=pl.BlockSpec((q, inner), lambda c: (c, 0)),
        out_shape=jax.ShapeDtypeStruct((m, inner), BF16),
        scratch_shapes=[pltpu.VMEM((SSD_G, q, GW), F32),
                        pltpu.VMEM((SSD_G, q, SSD_N), F32),
                        pltpu.VMEM((SSD_G, q, SSD_N), F32),
                        pltpu.VMEM((SSD_G, SSD_N, GW), F32),
                        pltpu.VMEM((SSD_G, q, GW), F32),
                        pltpu.VMEM((SSD_G, q, LANES), F32),
                        pltpu.VMEM((LANES, q), F32),
                        pltpu.VMEM((LANES, q), F32)],
        compiler_params=_params(("arbitrary",)),
        name=name,
    )(z, xbc, dt, dtb, alog, dsk, ssd_norm.reshape(1, inner))


def _gelu_tanh(x):
    return 0.5 * x * (1.0 + jnp.tanh(math.sqrt(2.0 / math.pi) * (x + 0.044715 * (x * x * x))))


def _s5_kernel(u_ref, wdup_ref, vtdup_ref, lq_ref, dsk_ref, o_ref,
               t_ref, w_ref, vt_ref, ucat_ref, s_ref, h_ref, yscr_ref):
    nc = ucat_ref.shape[0]
    kq = t_ref.shape[0]
    half = s_ref.shape[1] // 2

    @pl.when((pl.program_id(0) == 0) & (pl.program_id(1) == 0))
    def _():
        for s in range(1, S5_Q):
            t_ref[s * LANES:(s + 1) * LANES, 0:s * LANES] = jnp.zeros((LANES, s * LANES), BF16)

    @pl.when(pl.program_id(1) == 0)
    def _():
        h_ref[...] = jnp.zeros_like(h_ref)
        kv = vtdup_ref.shape[0]
        grp = (lax.broadcasted_iota(jnp.int32, (kv, LANES), 0) // S5_C) % S5_GB
        lane_half = lax.broadcasted_iota(jnp.int32, (kv, LANES), 1) // S5_P
        per_tile = LANES // S5_P
        tiles = S5_GB // per_tile
        for k in range(2 * tiles):
            ri = k // tiles
            keep = grp == (k % tiles) * per_tile + lane_half
            src = slice(ri * LANES, (ri + 1) * LANES)
            dst = slice(k * LANES, (k + 1) * LANES)
            w_ref[:, dst] = jnp.where(keep[:kq], wdup_ref[:, src], 0.0).astype(BF16)
            vt_ref[:, dst] = jnp.where(keep, vtdup_ref[:, src], 0.0).astype(BF16)
        k_all = lax.dot_general(w_ref[kq - LANES:kq, :], vt_ref[0:kq, :], (((1,), (1,)), ((), ())),
                                preferred_element_type=F32).astype(BF16)
        for s in range(S5_Q):
            for t in range(s, S5_Q):
                t_ref[s * LANES:(s + 1) * LANES, t * LANES:(t + 1) * LANES] = (
                    k_all[:, (t - s) * LANES:(t - s + 1) * LANES])

    for s in range(S5_Q):
        ucat_ref[:, s * LANES:(s + 1) * LANES] = u_ref[pl.ds(s, nc, stride=S5_Q), :].astype(BF16)
    ucat = ucat_ref[...]
    s_ref[...] = jnp.dot(ucat, w_ref[...], preferred_element_type=F32)
    l_re = lq_ref[:, :half]
    l_im = lq_ref[:, half:]

    def step(c, carry):
        h_re, h_im = carry
        srow = s_ref[pl.ds(c, 1), :]
        s_ref[pl.ds(c, 1), :] = jnp.concatenate([h_re, h_im], axis=1)
        n_re = l_re * h_re - l_im * h_im + srow[:, :half]
        n_im = l_re * h_im + l_im * h_re + srow[:, half:]
        return n_re, n_im

    h_re, h_im = lax.fori_loop(0, nc, step, (h_ref[0:1, :half], h_ref[0:1, half:]))
    h_ref[0:1, :half] = h_re
    h_ref[0:1, half:] = h_im

    hprev = s_ref[...].astype(BF16)
    parts = 4
    step = kq // parts
    for part in range(parts):
        c0, c1 = part * step, (part + 1) * step
        y = (jnp.dot(ucat_ref[:, 0:c1], t_ref[0:c1, c0:c1], preferred_element_type=F32)
             + lax.dot_general(hprev, vt_ref[LANES + c0:LANES + c1, :], (((1,), (1,)), ((), ())),
                               preferred_element_type=F32))
        for s in range(c0 // LANES, c1 // LANES):
            us = u_ref[pl.ds(s, nc, stride=S5_Q), :]
            yb = y[:, s * LANES - c0:(s + 1) * LANES - c0] + dsk_ref[...] * us
            yscr_ref[pl.ds(s, nc, stride=S5_Q), :] = _gelu_tanh(yb)
    o_ref[...] = yscr_ref[...].astype(o_ref.dtype)


def _s5_matrices(lam_re, lam_im, log_step, b_re, b_im, c_re, c_im):
    g = lam_re.shape[0]
    nb = g // S5_GB
    qn = S5_Q
    step = jnp.exp(log_step)[:, None]
    mag = jnp.exp(lam_re * step)
    ang = lam_im * step
    lb_re, lb_im = mag * jnp.cos(ang), mag * jnp.sin(ang)
    denom = lam_re * lam_re + lam_im * lam_im
    nr, ni = lb_re - 1.0, lb_im
    f_re = (nr * lam_re + ni * lam_im) / denom
    f_im = (ni * lam_re - nr * lam_im) / denom
    bb_re = f_re[..., None] * b_re - f_im[..., None] * b_im
    bb_im = f_re[..., None] * b_im + f_im[..., None] * b_re
    pw_re = [jnp.ones_like(lb_re)]
    pw_im = [jnp.zeros_like(lb_im)]
    for _ in range(qn):
        r, i = pw_re[-1], pw_im[-1]
        pw_re.append(r * lb_re - i * lb_im)
        pw_im.append(r * lb_im + i * lb_re)
    pw_re = jnp.stack(pw_re)
    pw_im = jnp.stack(pw_im)
    def per_block(re, im):
        def rows(v):
            nq = v.shape[0]
            v = v.reshape(nq, nb, S5_GB, S5_C, S5_P).transpose(1, 0, 2, 3, 4)
            return v.reshape(nb, nq * LANES, S5_P)
        re, im = rows(re), rows(im)
        return jnp.concatenate([re, re, im, im], axis=-1)

    bt_re = bb_re.transpose(0, 2, 1)[None]
    bt_im = bb_im.transpose(0, 2, 1)[None]
    ps_re = pw_re[:qn][::-1][:, :, None, :]
    ps_im = pw_im[:qn][::-1][:, :, None, :]
    wdup = per_block(ps_re * bt_re - ps_im * bt_im, ps_re * bt_im + ps_im * bt_re)
    m_re = c_re[None] * pw_re[:, :, None, :] - c_im[None] * pw_im[:, :, None, :]
    m_im = c_re[None] * pw_im[:, :, None, :] + c_im[None] * pw_re[:, :, None, :]
    vtdup = per_block(m_re, -m_im)
    lq = jnp.concatenate([pw_re[qn].reshape(nb, 1, S5_GB * S5_P),
                          pw_im[qn].reshape(nb, 1, S5_GB * S5_P)], axis=-1)
    return wdup, vtdup, lq


def _s5(u, mats, d_s5, name):
    m, width = u.shape
    wdup, vtdup, lq = mats
    nb = width // LANES
    rows = m // S5_ROW_BLOCKS
    nc = rows // S5_Q
    kq = S5_Q * LANES
    kv = vtdup.shape[1]
    ks = lq.shape[2]
    dup = wdup.shape[2]
    return pl.pallas_call(
        _s5_kernel,
        grid=(nb, S5_ROW_BLOCKS),
        in_specs=[pl.BlockSpec((rows, LANES), lambda b, r: (r, b)),
                  pl.BlockSpec((None, kq, dup), lambda b, r: (b, 0, 0)),
                  pl.BlockSpec((None, kv, dup), lambda b, r: (b, 0, 0)),
                  pl.BlockSpec((None, 1, ks), lambda b, r: (b, 0, 0)),
                  pl.BlockSpec((1, LANES), lambda b, r: (0, b))],
        out_specs=pl.BlockSpec((rows, LANES), lambda b, r: (r, b)),
        out_shape=jax.ShapeDtypeStruct((m, width), BF16),
        scratch_shapes=[pltpu.VMEM((kq, kq), BF16),
                        pltpu.VMEM((kq, ks), BF16),
                        pltpu.VMEM((kv, ks), BF16),
                        pltpu.VMEM((nc, kq), BF16),
                        pltpu.VMEM((nc, ks), F32),
                        pltpu.VMEM((8, ks), F32),
                        pltpu.VMEM((rows, LANES), F32)],
        compiler_params=_params(("arbitrary", "arbitrary")),
        name=name,
    )(u, wdup, vtdup, lq, d_s5.reshape(1, width))


def kernel(x, meta_tokens, norm_mix, w_in, conv_w, conv_b, dt_bias, a_log, d_ssd, ssd_norm,
           w_ssd_up, lam_re, lam_im, log_step, b_re, b_im, c_re, c_im, d_s5, w_glu, w_out,
           norm_ffn, w_gate_up, w_down, norm_final):
    batch, seq, d = x.shape
    depth = w_in.shape[0]
    heads = dt_bias.shape[1]
    inner = heads * SSD_P
    cdim = conv_w.shape[2]
    s5w = d_s5.shape[1]
    off_xbc, off_dt = inner, inner + cdim
    off_u = off_dt + heads
    off_g = off_u + s5w
    length = N_META + seq
    lp = -(-length // ROW_ALIGN) * ROW_ALIGN

    w_up_b = w_ssd_up.astype(BF16)
    w_glu_b = w_glu.astype(BF16)
    w_down_b = w_down.astype(BF16)

    outs = []
    for b in range(batch):
        h = jnp.concatenate([meta_tokens.astype(F32), x[b], jnp.zeros((lp - length, d), F32)], axis=0)
        for l in range(depth):
            wl = w_in[l]
            w_z = wl[:, :off_xbc].astype(BF16)
            w_xbc = wl[:, off_xbc:off_dt].astype(BF16)
            w_dt = jnp.pad(wl[:, off_dt:off_u], ((0, 0), (0, LANES - heads))).astype(BF16)
            w_u = wl[:, off_u:off_g].astype(BF16)
            w_g = wl[:, off_g:].astype(BF16)

            hn = _rmsnorm(h, norm_mix[l], BF16, f"norm_mix{l}")
            z = _mm(hn, w_z, F32, f"proj_z{l}")
            xbc = _mm_conv(hn, w_xbc, conv_w[l], conv_b[l], f"proj_xbc{l}")
            dt = _mm(hn, w_dt, F32, f"proj_dt{l}")
            u = _mm(hn, w_u, F32, f"proj_u{l}")
            gates = _mm(hn, w_g, F32, f"proj_gates{l}", act="sigmoid")

            y_a = _ssd(z, xbc, dt, dt_bias[l], a_log[l], d_ssd[l], ssd_norm[l], f"ssd{l}")
            mats = _s5_matrices(lam_re[l], lam_im[l], log_step[l], b_re[l], b_im[l], c_re[l], c_im[l])
            gy = _s5(u, mats, d_s5[l], f"s5{l}")

            merged = _merge(y_a, gy, w_up_b, w_glu_b, l, gates, f"merge{l}")
            h = _mm_res(merged, w_out, h, f"out_proj{l}", 1376, 512, layer=l)

            hn = _rmsnorm(h, norm_ffn[l], BF16, f"norm_ffn{l}")
            hidden = _swiglu(hn, w_gate_up, l, f"swiglu{l}")
            h = _mm_res(hidden, w_down_b, h, f"down_proj{l}", 688, 256, layer=l)
        out = _rmsnorm(h, norm_final, x.dtype, "norm_final")
        outs.append(out[N_META:length])
    return jnp.stack(outs, axis=0)
```

```python
import functools
import math

import jax
import jax.numpy as jnp
from jax import lax
from jax.experimental import pallas as pl
from jax.experimental.pallas import tpu as pltpu

F32 = jnp.float32
BF16 = jnp.bfloat16

N_META = 16
EPS = 1e-6
SSD_P = 64
SSD_N = 128
SSD_G = 8
SSD_J = 8
SSD_Q = 128
SSD_CONV = 4
GW = SSD_J * SSD_P
S5_C = 16
S5_P = 64
S5_Q = 16
S5_GB = 8
S5_ROW_BLOCKS = 3
LANES = 128
ROW_ALIGN = S5_Q * 8 * S5_ROW_BLOCKS
VMEM_LIMIT = 56 * 1024 * 1024


def _pick(n, cap, align=16):
    best = None
    for d in range(align, min(n, cap) + 1, align):
        if n % d == 0:
            best = d
    assert best is not None, (n, cap)
    return best


def _params(sem):
    return pltpu.CompilerParams(dimension_semantics=sem, vmem_limit_bytes=VMEM_LIMIT)


def _rmsnorm_kernel(x_ref, w_ref, o_ref):
    x = x_ref[...]
    ms = jnp.mean(x * x, axis=-1, keepdims=True)
    o_ref[...] = (x * lax.rsqrt(ms + EPS) * w_ref[...]).astype(o_ref.dtype)


def _rmsnorm(h, w, out_dtype, name, row0=0):
    m, d = h.shape
    m_out = m - row0
    bm = _pick(math.gcd(m_out, row0) if row0 else m_out, 384)
    first = row0 // bm
    return pl.pallas_call(
        _rmsnorm_kernel,
        grid=(m_out // bm,),
        in_specs=[pl.BlockSpec((bm, d), lambda i: (i + first, 0)),
                  pl.BlockSpec((1, d), lambda i: (0, 0))],
        out_specs=pl.BlockSpec((bm, d), lambda i: (i, 0)),
        out_shape=jax.ShapeDtypeStruct((m_out, d), out_dtype),
        compiler_params=_params(("arbitrary",)),
        name=name,
    )(h, w.reshape(1, d))


def _mm_kernel(x_ref, w_ref, o_ref, *, act):
    acc = jnp.dot(x_ref[...], w_ref[...].astype(BF16), preferred_element_type=F32)
    if act == "sigmoid":
        acc = jax.nn.sigmoid(acc)
    o_ref[...] = acc.astype(o_ref.dtype)


def _w_spec(w, k, bn, layer, col0):
    if layer is None:
        return pl.BlockSpec((k, bn), lambda i, j: (0, j))
    assert col0 % bn == 0, (col0, bn)
    return pl.BlockSpec((None, k, bn), lambda i, j: (layer, 0, col0 // bn + j))


def _mm(x, w, out_dtype, name, act=None, bm_cap=1376, bn_cap=512, layer=None, col0=0, n=None):
    m, k = x.shape
    n = w.shape[1] if layer is None else n
    bm = _pick(m, bm_cap)
    bn = _pick(n, bn_cap, LANES)
    return pl.pallas_call(
        functools.partial(_mm_kernel, act=act),
        grid=(m // bm, n // bn),
        in_specs=[pl.BlockSpec((bm, k), lambda i, j: (i, 0)),
                  _w_spec(w, k, bn, layer, col0)],
        out_specs=pl.BlockSpec((bm, bn), lambda i, j: (i, j)),
        out_shape=jax.ShapeDtypeStruct((m, n), out_dtype),
        compiler_params=_params(("arbitrary", "arbitrary")),
        name=name,
    )(x, w)


def _conv_silu(acc, tail, cw_ref, cb_ref):
    rows8 = lax.broadcasted_iota(jnp.int32, tail.shape, 0)
    w_last = cw_ref[SSD_CONV - 1:SSD_CONV, :]
    conv = cb_ref[...] + w_last * acc
    top = cb_ref[...] + w_last * acc[0:8]
    for k in range(SSD_CONV - 1):
        sh = SSD_CONV - 1 - k
        w_k = cw_ref[k:k + 1, :]
        shifted = pltpu.roll(acc, sh, axis=0)
        conv = conv + w_k * shifted
        top = top + w_k * jnp.where(rows8 < sh, pltpu.roll(tail, sh, axis=0), shifted[0:8])
    return conv * jax.nn.sigmoid(conv), top * jax.nn.sigmoid(top)


def _mm_conv_kernel(x_ref, w_ref, cw_ref, cb_ref, o_ref, tail_ref):
    i = pl.program_id(0)
    j = pl.program_id(1)
    bm = o_ref.shape[0]

    @pl.when(i == 0)
    def _():
        tail_ref[j] = jnp.zeros(tail_ref.shape[1:], F32)

    acc = jnp.dot(x_ref[...], w_ref[...].astype(BF16), preferred_element_type=F32)
    full, top = _conv_silu(acc, tail_ref[j], cw_ref, cb_ref)
    o_ref[...] = full
    o_ref[0:8, :] = top
    tail_ref[j] = acc[bm - 8:bm]


def _mm_conv(x, w, conv_w, conv_b, name, bm_cap=1376, bn_cap=512, layer=None, col0=0):
    m, k = x.shape
    n = conv_w.shape[1]
    bm = _pick(m, bm_cap)
    bn = _pick(n, bn_cap, LANES)
    return pl.pallas_call(
        _mm_conv_kernel,
        grid=(m // bm, n // bn),
        in_specs=[pl.BlockSpec((bm, k), lambda i, j: (i, 0)),
                  _w_spec(w, k, bn, layer, col0),
                  pl.BlockSpec((SSD_CONV, bn), lambda i, j: (0, j)),
                  pl.BlockSpec((1, bn), lambda i, j: (0, j))],
        out_specs=pl.BlockSpec((bm, bn), lambda i, j: (i, j)),
        out_shape=jax.ShapeDtypeStruct((m, n), F32),
        scratch_shapes=[pltpu.VMEM((n // bn, 8, bn), F32)],
        compiler_params=_params(("arbitrary", "arbitrary")),
        name=name,
    )(x, w, conv_w, conv_b.reshape(1, n))


def _mm_res_kernel(x_ref, w_ref, r_ref, o_ref):
    o_ref[...] = r_ref[...] + jnp.dot(x_ref[...], w_ref[...].astype(BF16), preferred_element_type=F32)


def _mm_res(x, w, res, name, bm_cap, bn_cap, layer=None):
    m, k = x.shape
    n = w.shape[-1]
    bm = _pick(m, bm_cap)
    bn = _pick(n, bn_cap, LANES)
    return pl.pallas_call(
        _mm_res_kernel,
        grid=(m // bm, n // bn),
        in_specs=[pl.BlockSpec((bm, k), lambda i, j: (i, 0)),
                  _w_spec(w, k, bn, layer, 0),
                  pl.BlockSpec((bm, bn), lambda i, j: (i, j))],
        out_specs=pl.BlockSpec((bm, bn), lambda i, j: (i, j)),
        out_shape=jax.ShapeDtypeStruct((m, n), F32),
        input_output_aliases={2: 0},
        compiler_params=_params(("arbitrary", "arbitrary")),
        name=name,
    )(x, w, res)


def _swiglu_kernel(x_ref, wg_ref, wu_ref, o_ref):
    x = x_ref[...]
    g = jnp.dot(x, wg_ref[...].astype(BF16), preferred_element_type=F32)
    u = jnp.dot(x, wu_ref[...].astype(BF16), preferred_element_type=F32)
    o_ref[...] = (g * jax.nn.sigmoid(g) * u).astype(o_ref.dtype)


def _swiglu(x, w_gate_up, layer, name):
    m, k = x.shape
    f = w_gate_up.shape[2] // 2
    bm = _pick(m, 1376)
    bn = _pick(f, 512, LANES)
    nb = f // bn
    return pl.pallas_call(
        _swiglu_kernel,
        grid=(m // bm, nb),
        in_specs=[pl.BlockSpec((bm, k), lambda i, j: (i, 0)),
                  pl.BlockSpec((None, k, bn), lambda i, j: (layer, 0, j)),
                  pl.BlockSpec((None, k, bn), lambda i, j: (layer, 0, j + nb))],
        out_specs=pl.BlockSpec((bm, bn), lambda i, j: (i, j)),
        out_shape=jax.ShapeDtypeStruct((m, f), BF16),
        compiler_params=_params(("arbitrary", "arbitrary")),
        name=name,
    )(x, w_gate_up, w_gate_up)


def _merge_kernel(ya_ref, gy_ref, wup_ref, wg1_ref, wg2_ref, ga_ref, gb_ref, o_ref):
    a = jnp.dot(ya_ref[...], wup_ref[...], preferred_element_type=F32)
    gy = gy_ref[...]
    g1 = jnp.dot(gy, wg1_ref[...], preferred_element_type=F32)
    g2 = jnp.dot(gy, wg2_ref[...], preferred_element_type=F32)
    o_ref[...] = (ga_ref[...] * a + gb_ref[...] * (g1 * jax.nn.sigmoid(g2))).astype(o_ref.dtype)


def _merge(ya, gy, w_up, w_glu, layer, gates, name):
    m, d = ya.shape
    k2 = gy.shape[1]
    bm = _pick(m, 688)
    bn = _pick(d, 512, LANES)
    nb = d // bn
    return pl.pallas_call(
        _merge_kernel,
        grid=(m // bm, nb),
        in_specs=[pl.BlockSpec((bm, d), lambda i, j: (i, 0)),
                  pl.BlockSpec((bm, k2), lambda i, j: (i, 0)),
                  pl.BlockSpec((None, d, bn), lambda i, j: (layer, 0, j)),
                  pl.BlockSpec((None, k2, bn), lambda i, j: (layer, 0, j)),
                  pl.BlockSpec((None, k2, bn), lambda i, j: (layer, 0, j + nb)),
                  pl.BlockSpec((bm, bn), lambda i, j: (i, j)),
                  pl.BlockSpec((bm, bn), lambda i, j: (i, j + nb))],
        out_specs=pl.BlockSpec((bm, bn), lambda i, j: (i, j)),
        out_shape=jax.ShapeDtypeStruct((m, d), BF16),
        compiler_params=_params(("arbitrary", "arbitrary")),
        name=name,
    )(ya, gy, w_up, w_glu, w_glu, gates, gates)


def _softplus(x):
    return jnp.maximum(x, 0.0) + jnp.log1p(jnp.exp(-jnp.abs(x)))


def _ssd_kernel(z_ref, xbc_ref, dt_ref, dtb_ref, alog_ref, dsk_ref, nw_ref,
                y_ref,
                xs_ref, bm_ref, cm_ref, st_ref, ya_ref, acsg_ref, acst_ref, dtt_ref, *, pad):
    q = SSD_Q
    c = pl.program_id(0)

    @pl.when(c == 0)
    def _():
        st_ref[...] = jnp.zeros_like(st_ref)

    n_x = SSD_G * GW
    for g in range(SSD_G):
        xs_ref[g] = xbc_ref[:, g * GW:(g + 1) * GW]
        bm_ref[g] = xbc_ref[:, n_x + g * SSD_N:n_x + (g + 1) * SSD_N]
        cm_ref[g] = xbc_ref[:, n_x + (SSD_G + g) * SSD_N:n_x + (SSD_G + g + 1) * SSD_N]

    if pad:
        @pl.when(c == 0)
        def _():
            for g in range(SSD_G):
                xs_ref[g, 0:pad, :] = jnp.zeros((pad, GW), F32)

    dt = _softplus(dt_ref[...] + dtb_ref[...])
    acs = dt * (-jnp.exp(alog_ref[...]))
    rows = lax.broadcasted_iota(jnp.int32, (q, LANES), 0)
    lanes = lax.broadcasted_iota(jnp.int32, (q, LANES), 1)
    sh = 1
    while sh < q:
        acs = acs + jnp.where(rows >= sh, pltpu.roll(acs, sh, axis=0), 0.0)
        sh *= 2
    acs = acs * math.log2(math.e)
    acst_ref[...] = acs.T
    dtt_ref[...] = dt.T
    for g in range(SSD_G):
        acsg_ref[g] = acs if g == 0 else pltpu.roll(acs, LANES - SSD_J * g, axis=1)
    causal = rows >= lanes

    left = lanes < SSD_P

    def group_body(g, carry):
        b_g = bm_ref[g]
        c_g = cm_ref[g]
        cb = lax.dot_general(c_g.astype(BF16), b_g.astype(BF16), (((1,), (1,)), ((), ())),
                             preferred_element_type=F32)
        b_t = b_g.T
        a_blk = acsg_ref[g]
        g8 = pl.multiple_of(g * SSD_J, SSD_J)
        a_t = acst_ref[pl.ds(g8, SSD_J), :]
        d_t = dtt_ref[pl.ds(g8, SSD_J), :]
        none = jnp.zeros((SSD_N, SSD_N), F32)
        for pr in range(SSD_J // 2):
            pcols = slice(pr * LANES, (pr + 1) * LANES)
            x_p = xs_ref[g, :, pcols]
            s_p = st_ref[g, :, pcols]
            out_rows, upd_rows, ends = [], [], []
            for j in (2 * pr, 2 * pr + 1):
                col = jnp.broadcast_to(a_blk[:, j:j + 1], (q, LANES))
                row = a_t[j:j + 1, :]
                drow = d_t[j:j + 1, :]
                out_rows.append(jnp.where(causal, jnp.exp2(col - row), 0.0) * (cb * drow))
                out_rows.append(jnp.exp2(col) * c_g)
                last = row[:, q - 1:q]
                upd_rows.append(b_t * (drow * jnp.exp2(last - row)))
                upd_rows.append(none)
                ends.append(jnp.exp2(last))
            x_l = jnp.where(left, x_p, 0.0)
            x_r = x_p - x_l
            s_l = jnp.where(left, s_p, 0.0)
            s_r = s_p - s_l
            lhs = jnp.concatenate([jnp.concatenate(out_rows, axis=1),
                                   jnp.concatenate(upd_rows, axis=1)], axis=0).astype(BF16)
            rhs = jnp.concatenate([x_l, s_l, x_r, s_r], axis=0).astype(BF16)
            res = jnp.dot(lhs, rhs, preferred_element_type=F32)
            ya_ref[g, :, pcols] = res[0:q]
            decay = jnp.where(left[0:1, :], ends[0], ends[1])
            st_ref[g, :, pcols] = s_p * decay + res[q:]
        return carry

    lax.fori_loop(0, SSD_G, group_body, 0, unroll=True)

    for g in range(SSD_G):
        cols = slice(g * GW, (g + 1) * GW)
        y = ya_ref[g] + xs_ref[g] * dsk_ref[:, cols]
        zz = z_ref[:, cols]
        y = y * (zz * jax.nn.sigmoid(zz))
        ms = jnp.mean(y * y, axis=-1, keepdims=True)
        y_ref[:, cols] = (y * lax.rsqrt(ms + EPS) * nw_ref[:, cols]).astype(y_ref.dtype)


def _ssd(z, xbc, dt, dt_bias, a_log, d_ssd, ssd_norm, name, pad=0):
    m, inner = z.shape
    cdim = xbc.shape[1]
    heads = dt_bias.shape[0]
    q = SSD_Q
    assert 0 <= pad < q and pad % 8 == 0, pad
    dtb = jnp.pad(dt_bias, (0, LANES - heads)).reshape(1, LANES)
    alog = jnp.pad(a_log, (0, LANES - heads)).reshape(1, LANES)
    dsk = jnp.repeat(d_ssd, SSD_P).reshape(1, inner)
    full = lambda r, w: pl.BlockSpec((r, w), lambda c: (0, 0))
    return pl.pallas_call(
        functools.partial(_ssd_kernel, pad=pad),
        grid=(m // q,),
        in_specs=[pl.BlockSpec((q, inner), lambda c: (c, 0)),
                  pl.BlockSpec((q, cdim), lambda c: (c, 0)),
                  pl.BlockSpec((q, LANES), lambda c: (c, 0)),
                  full(1, LANES), full(1, LANES), full(1, inner), full(1, inner)],
        out_specs=pl.BlockSpec((q, inner), lambda c: (c, 0)),
        out_shape=jax.ShapeDtypeStruct((m, inner), BF16),
        scratch_shapes=[pltpu.VMEM((SSD_G, q, GW), F32),
                        pltpu.VMEM((SSD_G, q, SSD_N), F32),
                        pltpu.VMEM((SSD_G, q, SSD_N), F32),
                        pltpu.VMEM((SSD_G, SSD_N, GW), F32),
                        pltpu.VMEM((SSD_G, q, GW), F32),
                        pltpu.VMEM((SSD_G, q, LANES), F32),
                        pltpu.VMEM((LANES, q), F32),
                        pltpu.VMEM((LANES, q), F32)],
        compiler_params=_params(("arbitrary",)),
        name=name,
    )(z, xbc, dt, dtb, alog, dsk, ssd_norm.reshape(1, inner))


def _gelu_tanh(x):
    return 0.5 * x * (1.0 + jnp.tanh(math.sqrt(2.0 / math.pi) * (x + 0.044715 * (x * x * x))))


def _s5_kernel(u_ref, wdup_ref, vtdup_ref, lq_ref, dsk_ref, o_ref,
               t_ref, w_ref, vt_ref, ucat_ref, s_ref, h_ref, yscr_ref):
    nc = ucat_ref.shape[0]
    kq = t_ref.shape[0]
    half = s_ref.shape[1] // 2

    @pl.when((pl.program_id(0) == 0) & (pl.program_id(1) == 0))
    def _():
        for s in range(1, S5_Q):
            t_ref[s * LANES:(s + 1) * LANES, 0:s * LANES] = jnp.zeros((LANES, s * LANES), BF16)

    @pl.when(pl.program_id(1) == 0)
    def _():
        h_ref[...] = jnp.zeros_like(h_ref)
        kv = vtdup_ref.shape[0]
        grp = (lax.broadcasted_iota(jnp.int32, (kv, LANES), 0) // S5_C) % S5_GB
        lane_half = lax.broadcasted_iota(jnp.int32, (kv, LANES), 1) // S5_P
        per_tile = LANES // S5_P
        tiles = S5_GB // per_tile
        for k in range(2 * tiles):
            ri = k // tiles
            keep = grp == (k % tiles) * per_tile + lane_half
            src = slice(ri * LANES, (ri + 1) * LANES)
            dst = slice(k * LANES, (k + 1) * LANES)
            w_ref[:, dst] = jnp.where(keep[:kq], wdup_ref[:, src], 0.0).astype(BF16)
            vt_ref[:, dst] = jnp.where(keep, vtdup_ref[:, src], 0.0).astype(BF16)
        k_all = lax.dot_general(w_ref[kq - LANES:kq, :], vt_ref[0:kq, :], (((1,), (1,)), ((), ())),
                                preferred_element_type=F32).astype(BF16)
        for s in range(S5_Q):
            for t in range(s, S5_Q):
                t_ref[s * LANES:(s + 1) * LANES, t * LANES:(t + 1) * LANES] = (
                    k_all[:, (t - s) * LANES:(t - s + 1) * LANES])

    for s in range(S5_Q):
        ucat_ref[:, s * LANES:(s + 1) * LANES] = u_ref[pl.ds(s, nc, stride=S5_Q), :].astype(BF16)
    ucat = ucat_ref[...]
    s_ref[...] = jnp.dot(ucat, w_ref[...], preferred_element_type=F32)
    l_re = lq_ref[:, :half]
    l_im = lq_ref[:, half:]

    parts = 4
    width = kq // parts
    intra = [jnp.dot(ucat_ref[:, 0:(part + 1) * width], t_ref[0:(part + 1) * width, part * width:(part + 1) * width],
                     preferred_element_type=F32) for part in range(parts)]

    h_re, h_im = h_ref[0:1, :half], h_ref[0:1, half:]
    for c in range(nc):
        srow = s_ref[c:c + 1, :]
        s_ref[c:c + 1, :] = jnp.concatenate([h_re, h_im], axis=1)
        h_re, h_im = (l_re * h_re - l_im * h_im + srow[:, :half],
                      l_re * h_im + l_im * h_re + srow[:, half:])
    h_ref[0:1, :half] = h_re
    h_ref[0:1, half:] = h_im

    hprev = s_ref[...].astype(BF16)
    for part in range(parts):
        c0, c1 = part * width, (part + 1) * width
        y = intra[part] + lax.dot_general(hprev, vt_ref[LANES + c0:LANES + c1, :], (((1,), (1,)), ((), ())),
                                          preferred_element_type=F32)
        for s in range(c0 // LANES, c1 // LANES):
            us = u_ref[pl.ds(s, nc, stride=S5_Q), :]
            yb = y[:, s * LANES - c0:(s + 1) * LANES - c0] + dsk_ref[...] * us
            yscr_ref[pl.ds(s, nc, stride=S5_Q), :] = _gelu_tanh(yb)
    o_ref[...] = yscr_ref[...].astype(o_ref.dtype)


def _s5_matrices(lam_re, lam_im, log_step, b_re, b_im, c_re, c_im):
    g = lam_re.shape[0]
    nb = g // S5_GB
    qn = S5_Q
    step = jnp.exp(log_step)[:, None]
    mag = jnp.exp(lam_re * step)
    ang = lam_im * step
    lb_re, lb_im = mag * jnp.cos(ang), mag * jnp.sin(ang)
    denom = lam_re * lam_re + lam_im * lam_im
    nr, ni = lb_re - 1.0, lb_im
    f_re = (nr * lam_re + ni * lam_im) / denom
    f_im = (ni * lam_re - nr * lam_im) / denom
    bb_re = f_re[..., None] * b_re - f_im[..., None] * b_im
    bb_im = f_re[..., None] * b_im + f_im[..., None] * b_re
    pw_re = [jnp.ones_like(lb_re)]
    pw_im = [jnp.zeros_like(lb_im)]
    for _ in range(qn):
        r, i = pw_re[-1], pw_im[-1]
        pw_re.append(r * lb_re - i * lb_im)
        pw_im.append(r * lb_im + i * lb_re)
    pw_re = jnp.stack(pw_re)
    pw_im = jnp.stack(pw_im)
    def per_block(re, im):
        def rows(v):
            nq = v.shape[0]
            v = v.reshape(nq, nb, S5_GB, S5_C, S5_P).transpose(1, 0, 2, 3, 4)
            return v.reshape(nb, nq * LANES, S5_P)
        re, im = rows(re), rows(im)
        return jnp.concatenate([re, re, im, im], axis=-1)

    bt_re = bb_re.transpose(0, 2, 1)[None]
    bt_im = bb_im.transpose(0, 2, 1)[None]
    ps_re = pw_re[:qn][::-1][:, :, None, :]
    ps_im = pw_im[:qn][::-1][:, :, None, :]
    wdup = per_block(ps_re * bt_re - ps_im * bt_im, ps_re * bt_im + ps_im * bt_re)
    m_re = c_re[None] * pw_re[:, :, None, :] - c_im[None] * pw_im[:, :, None, :]
    m_im = c_re[None] * pw_im[:, :, None, :] + c_im[None] * pw_re[:, :, None, :]
    vtdup = per_block(m_re, -m_im)
    lq = jnp.concatenate([pw_re[qn].reshape(nb, 1, S5_GB * S5_P),
                          pw_im[qn].reshape(nb, 1, S5_GB * S5_P)], axis=-1)
    return wdup, vtdup, lq


def _s5(u, mats, d_s5, name):
    m, width = u.shape
    wdup, vtdup, lq = mats
    nb = width // LANES
    rows = m // S5_ROW_BLOCKS
    nc = rows // S5_Q
    kq = S5_Q * LANES
    kv = vtdup.shape[1]
    ks = lq.shape[2]
    dup = wdup.shape[2]
    return pl.pallas_call(
        _s5_kernel,
        grid=(nb, S5_ROW_BLOCKS),
        in_specs=[pl.BlockSpec((rows, LANES), lambda b, r: (r, b)),
                  pl.BlockSpec((None, kq, dup), lambda b, r: (b, 0, 0)),
                  pl.BlockSpec((None, kv, dup), lambda b, r: (b, 0, 0)),
                  pl.BlockSpec((None, 1, ks), lambda b, r: (b, 0, 0)),
                  pl.BlockSpec((1, LANES), lambda b, r: (0, b))],
        out_specs=pl.BlockSpec((rows, LANES), lambda b, r: (r, b)),
        out_shape=jax.ShapeDtypeStruct((m, width), BF16),
        scratch_shapes=[pltpu.VMEM((kq, kq), BF16),
                        pltpu.VMEM((kq, ks), BF16),
                        pltpu.VMEM((kv, ks), BF16),
                        pltpu.VMEM((nc, kq), BF16),
                        pltpu.VMEM((nc, ks), F32),
                        pltpu.VMEM((8, ks), F32),
                        pltpu.VMEM((rows, LANES), F32)],
        compiler_params=_params(("arbitrary", "arbitrary")),
        name=name,
    )(u, wdup, vtdup, lq, d_s5.reshape(1, width))


def kernel(x, meta_tokens, norm_mix, w_in, conv_w, conv_b, dt_bias, a_log, d_ssd, ssd_norm,
           w_ssd_up, lam_re, lam_im, log_step, b_re, b_im, c_re, c_im, d_s5, w_glu, w_out,
           norm_ffn, w_gate_up, w_down, norm_final):
    batch, seq, d = x.shape
    depth = w_in.shape[0]
    heads = dt_bias.shape[1]
    inner = heads * SSD_P
    cdim = conv_w.shape[2]
    s5w = d_s5.shape[1]
    off_xbc, off_dt = inner, inner + cdim
    off_u = off_dt + heads
    off_g = off_u + s5w
    length = N_META + seq
    lp = -(-length // ROW_ALIGN) * ROW_ALIGN

    pad = lp - length

    w_up_b = w_ssd_up.astype(BF16)
    w_glu_b = w_glu.astype(BF16)
    w_down_b = w_down.astype(BF16)
    w_in_b = jnp.concatenate([w_in[:, :, :off_dt], w_in[:, :, off_u:], w_in[:, :, off_dt:off_u],
                              jnp.zeros((depth, d, LANES - heads), w_in.dtype)], axis=2).astype(BF16)
    col_u = off_dt
    col_g = col_u + s5w
    col_dt = col_g + 2 * d

    outs = []
    for b in range(batch):
        h = jnp.concatenate([jnp.zeros((pad, d), F32), meta_tokens.astype(F32), x[b]], axis=0)
        for l in range(depth):
            hn = _rmsnorm(h, norm_mix[l], BF16, f"norm_mix{l}")
            z = _mm(hn, w_in_b, F32, f"proj_z{l}", layer=l, col0=0, n=inner)
            xbc = _mm_conv(hn, w_in_b, conv_w[l], conv_b[l], f"proj_xbc{l}", layer=l, col0=off_xbc)
            dt = _mm(hn, w_in_b, F32, f"proj_dt{l}", layer=l, col0=col_dt, n=LANES)
            u = _mm(hn, w_in_b, F32, f"proj_u{l}", layer=l, col0=col_u, n=s5w)
            gates = _mm(hn, w_in_b, F32, f"proj_gates{l}", act="sigmoid", layer=l, col0=col_g, n=2 * d)

            y_a = _ssd(z, xbc, dt, dt_bias[l], a_log[l], d_ssd[l], ssd_norm[l], f"ssd{l}", pad=pad)
            mats = _s5_matrices(lam_re[l], lam_im[l], log_step[l], b_re[l], b_im[l], c_re[l], c_im[l])
            gy = _s5(u, mats, d_s5[l], f"s5{l}")

            merged = _merge(y_a, gy, w_up_b, w_glu_b, l, gates, f"merge{l}")
            h = _mm_res(merged, w_out, h, f"out_proj{l}", 1376, 512, layer=l)

            hn = _rmsnorm(h, norm_ffn[l], BF16, f"norm_ffn{l}")
            hidden = _swiglu(hn, w_gate_up, l, f"swiglu{l}")
            h = _mm_res(hidden, w_down_b, h, f"down_proj{l}", 688, 256, layer=l)
        outs.append(_rmsnorm(h, norm_final, x.dtype, "norm_final", row0=pad + N_META))
    return jnp.stack(outs, axis=0)
```

```python
import functools
import math

import jax
import jax.numpy as jnp
from jax import lax
from jax.experimental import pallas as pl
from jax.experimental.pallas import tpu as pltpu

F32 = jnp.float32
BF16 = jnp.bfloat16

N_META = 16
EPS = 1e-6
SSD_P = 64
SSD_N = 128
SSD_G = 8
SSD_J = 8
SSD_Q = 128
SSD_CONV = 4
GW = SSD_J * SSD_P
S5_C = 16
S5_P = 64
S5_Q = 16
S5_GB = 8
S5_ROW_BLOCKS = 3
LANES = 128
ROW_ALIGN = S5_Q * 8 * S5_ROW_BLOCKS
VMEM_LIMIT = 56 * 1024 * 1024


def _pick(n, cap, align=16):
    best = None
    for d in range(align, min(n, cap) + 1, align):
        if n % d == 0:
            best = d
    assert best is not None, (n, cap)
    return best


def _params(sem):
    return pltpu.CompilerParams(dimension_semantics=sem, vmem_limit_bytes=VMEM_LIMIT)


def _rmsnorm_kernel(x_ref, w_ref, o_ref):
    x = x_ref[...]
    ms = jnp.mean(x * x, axis=-1, keepdims=True)
    o_ref[...] = (x * lax.rsqrt(ms + EPS) * w_ref[...]).astype(o_ref.dtype)


def _rmsnorm(h, w, out_dtype, name, row0=0):
    m, d = h.shape
    m_out = m - row0
    bm = _pick(math.gcd(m_out, row0) if row0 else m_out, 384)
    first = row0 // bm
    return pl.pallas_call(
        _rmsnorm_kernel,
        grid=(m_out // bm,),
        in_specs=[pl.BlockSpec((bm, d), lambda i: (i + first, 0)),
                  pl.BlockSpec((1, d), lambda i: (0, 0))],
        out_specs=pl.BlockSpec((bm, d), lambda i: (i, 0)),
        out_shape=jax.ShapeDtypeStruct((m_out, d), out_dtype),
        compiler_params=_params(("arbitrary",)),
        name=name,
    )(h, w.reshape(1, d))


def _mm_kernel(x_ref, w_ref, o_ref, *, act):
    acc = jnp.dot(x_ref[...], w_ref[...].astype(BF16), preferred_element_type=F32)
    if act == "sigmoid":
        acc = jax.nn.sigmoid(acc)
    o_ref[...] = acc.astype(o_ref.dtype)


def _w_spec(w, k, bn, layer, col0):
    if layer is None:
        return pl.BlockSpec((k, bn), lambda i, j: (0, j))
    assert col0 % bn == 0, (col0, bn)
    return pl.BlockSpec((None, k, bn), lambda i, j: (layer, 0, col0 // bn + j))


def _mm(x, w, out_dtype, name, act=None, bm_cap=1376, bn_cap=512, layer=None, col0=0, n=None):
    m, k = x.shape
    n = w.shape[1] if layer is None else n
    bm = _pick(m, bm_cap)
    bn = _pick(n, bn_cap, LANES)
    return pl.pallas_call(
        functools.partial(_mm_kernel, act=act),
        grid=(m // bm, n // bn),
        in_specs=[pl.BlockSpec((bm, k), lambda i, j: (i, 0)),
                  _w_spec(w, k, bn, layer, col0)],
        out_specs=pl.BlockSpec((bm, bn), lambda i, j: (i, j)),
        out_shape=jax.ShapeDtypeStruct((m, n), out_dtype),
        compiler_params=_params(("arbitrary", "arbitrary")),
        name=name,
    )(x, w)


def _conv_silu(acc, tail, cw_ref, cb_ref):
    rows8 = lax.broadcasted_iota(jnp.int32, tail.shape, 0)
    w_last = cw_ref[SSD_CONV - 1:SSD_CONV, :]
    conv = cb_ref[...] + w_last * acc
    top = cb_ref[...] + w_last * acc[0:8]
    for k in range(SSD_CONV - 1):
        sh = SSD_CONV - 1 - k
        w_k = cw_ref[k:k + 1, :]
        shifted = pltpu.roll(acc, sh, axis=0)
        conv = conv + w_k * shifted
        top = top + w_k * jnp.where(rows8 < sh, pltpu.roll(tail, sh, axis=0), shifted[0:8])
    return conv * jax.nn.sigmoid(conv), top * jax.nn.sigmoid(top)


def _mm_conv_kernel(x_ref, w_ref, cw_ref, cb_ref, o_ref, tail_ref, *, pad):
    i = pl.program_id(0)
    j = pl.program_id(1)
    bm = o_ref.shape[0]

    @pl.when(i == 0)
    def _():
        tail_ref[j] = jnp.zeros(tail_ref.shape[1:], F32)

    acc = jnp.dot(x_ref[...], w_ref[...].astype(BF16), preferred_element_type=F32)
    full, top = _conv_silu(acc, tail_ref[j], cw_ref, cb_ref)
    o_ref[...] = full
    o_ref[0:8, :] = top
    tail_ref[j] = acc[bm - 8:bm]
    if pad:
        @pl.when(i == 0)
        def _():
            o_ref[0:pad, :] = jnp.zeros((pad, o_ref.shape[1]), F32)


def _mm_conv(x, w, conv_w, conv_b, name, bm_cap=1376, bn_cap=512, layer=None, col0=0, pad=0):
    m, k = x.shape
    n = conv_w.shape[1]
    bm = _pick(m, bm_cap)
    bn = _pick(n, bn_cap, LANES)
    assert 0 <= pad <= bm and pad % 8 == 0, pad
    return pl.pallas_call(
        functools.partial(_mm_conv_kernel, pad=pad),
        grid=(m // bm, n // bn),
        in_specs=[pl.BlockSpec((bm, k), lambda i, j: (i, 0)),
                  _w_spec(w, k, bn, layer, col0),
                  pl.BlockSpec((SSD_CONV, bn), lambda i, j: (0, j)),
                  pl.BlockSpec((1, bn), lambda i, j: (0, j))],
        out_specs=pl.BlockSpec((bm, bn), lambda i, j: (i, j)),
        out_shape=jax.ShapeDtypeStruct((m, n), F32),
        scratch_shapes=[pltpu.VMEM((n // bn, 8, bn), F32)],
        compiler_params=_params(("arbitrary", "arbitrary")),
        name=name,
    )(x, w, conv_w, conv_b.reshape(1, n))


def _mm_res_kernel(x_ref, w_ref, r_ref, o_ref):
    o_ref[...] = r_ref[...] + jnp.dot(x_ref[...], w_ref[...].astype(BF16), preferred_element_type=F32)


def _mm_res(x, w, res, name, bm_cap, bn_cap, layer=None):
    m, k = x.shape
    n = w.shape[-1]
    bm = _pick(m, bm_cap)
    bn = _pick(n, bn_cap, LANES)
    return pl.pallas_call(
        _mm_res_kernel,
        grid=(m // bm, n // bn),
        in_specs=[pl.BlockSpec((bm, k), lambda i, j: (i, 0)),
                  _w_spec(w, k, bn, layer, 0),
                  pl.BlockSpec((bm, bn), lambda i, j: (i, j))],
        out_specs=pl.BlockSpec((bm, bn), lambda i, j: (i, j)),
        out_shape=jax.ShapeDtypeStruct((m, n), F32),
        input_output_aliases={2: 0},
        compiler_params=_params(("arbitrary", "arbitrary")),
        name=name,
    )(x, w, res)


def _swiglu_kernel(x_ref, wg_ref, wu_ref, o_ref):
    x = x_ref[...]
    g = jnp.dot(x, wg_ref[...].astype(BF16), preferred_element_type=F32)
    u = jnp.dot(x, wu_ref[...].astype(BF16), preferred_element_type=F32)
    o_ref[...] = (g * jax.nn.sigmoid(g) * u).astype(o_ref.dtype)


def _swiglu(x, w_gate_up, layer, name):
    m, k = x.shape
    f = w_gate_up.shape[2] // 2
    bm = _pick(m, 1376)
    bn = _pick(f, 512, LANES)
    nb = f // bn
    return pl.pallas_call(
        _swiglu_kernel,
        grid=(m // bm, nb),
        in_specs=[pl.BlockSpec((bm, k), lambda i, j: (i, 0)),
                  pl.BlockSpec((None, k, bn), lambda i, j: (layer, 0, j)),
                  pl.BlockSpec((None, k, bn), lambda i, j: (layer, 0, j + nb))],
        out_specs=pl.BlockSpec((bm, bn), lambda i, j: (i, j)),
        out_shape=jax.ShapeDtypeStruct((m, f), BF16),
        compiler_params=_params(("arbitrary", "arbitrary")),
        name=name,
    )(x, w_gate_up, w_gate_up)


def _merge_kernel(ya_ref, gy_ref, wup_ref, wg1_ref, wg2_ref, ga_ref, gb_ref, o_ref):
    a = jnp.dot(ya_ref[...], wup_ref[...], preferred_element_type=F32)
    gy = gy_ref[...]
    g1 = jnp.dot(gy, wg1_ref[...], preferred_element_type=F32)
    g2 = jnp.dot(gy, wg2_ref[...], preferred_element_type=F32)
    o_ref[...] = (ga_ref[...] * a + gb_ref[...] * (g1 * jax.nn.sigmoid(g2))).astype(o_ref.dtype)


def _merge(ya, gy, w_up, w_glu, layer, gates, name):
    m, d = ya.shape
    k2 = gy.shape[1]
    bm = _pick(m, 688)
    bn = _pick(d, 512, LANES)
    nb = d // bn
    return pl.pallas_call(
        _merge_kernel,
        grid=(m // bm, nb),
        in_specs=[pl.BlockSpec((bm, d), lambda i, j: (i, 0)),
                  pl.BlockSpec((bm, k2), lambda i, j: (i, 0)),
                  pl.BlockSpec((None, d, bn), lambda i, j: (layer, 0, j)),
                  pl.BlockSpec((None, k2, bn), lambda i, j: (layer, 0, j)),
                  pl.BlockSpec((None, k2, bn), lambda i, j: (layer, 0, j + nb)),
                  pl.BlockSpec((bm, bn), lambda i, j: (i, j)),
                  pl.BlockSpec((bm, bn), lambda i, j: (i, j + nb))],
        out_specs=pl.BlockSpec((bm, bn), lambda i, j: (i, j)),
        out_shape=jax.ShapeDtypeStruct((m, d), BF16),
        compiler_params=_params(("arbitrary", "arbitrary")),
        name=name,
    )(ya, gy, w_up, w_glu, w_glu, gates, gates)


def _softplus(x):
    return jnp.maximum(x, 0.0) + jnp.log1p(jnp.exp(-jnp.abs(x)))


def _ssd_kernel(z_ref, xbc_ref, dt_ref, dtb_ref, alog_ref, dsk_ref, nw_ref,
                y_ref,
                xs_ref, bm_ref, cm_ref, st_ref, ya_ref, acsg_ref, acst_ref, dtt_ref):
    q = SSD_Q
    c = pl.program_id(0)

    @pl.when(c == 0)
    def _():
        st_ref[...] = jnp.zeros_like(st_ref)

    n_x = SSD_G * GW
    for g in range(SSD_G):
        xs_ref[g] = xbc_ref[:, g * GW:(g + 1) * GW]
        bm_ref[g] = xbc_ref[:, n_x + g * SSD_N:n_x + (g + 1) * SSD_N]
        cm_ref[g] = xbc_ref[:, n_x + (SSD_G + g) * SSD_N:n_x + (SSD_G + g + 1) * SSD_N]

    dt = _softplus(dt_ref[...] + dtb_ref[...])
    acs = dt * (-jnp.exp(alog_ref[...]))
    rows = lax.broadcasted_iota(jnp.int32, (q, LANES), 0)
    lanes = lax.broadcasted_iota(jnp.int32, (q, LANES), 1)
    sh = 1
    while sh < q:
        acs = acs + jnp.where(rows >= sh, pltpu.roll(acs, sh, axis=0), 0.0)
        sh *= 2
    acs = acs * math.log2(math.e)
    acst_ref[...] = acs.T
    dtt_ref[...] = dt.T
    for g in range(SSD_G):
        acsg_ref[g] = acs if g == 0 else pltpu.roll(acs, LANES - SSD_J * g, axis=1)
    causal = rows >= lanes

    left = lanes < SSD_P

    def group_body(g, carry):
        b_g = bm_ref[g]
        c_g = cm_ref[g]
        cb = lax.dot_general(c_g.astype(BF16), b_g.astype(BF16), (((1,), (1,)), ((), ())),
                             preferred_element_type=F32)
        b_t = b_g.T
        a_blk = acsg_ref[g]
        g8 = pl.multiple_of(g * SSD_J, SSD_J)
        a_t = acst_ref[pl.ds(g8, SSD_J), :]
        d_t = dtt_ref[pl.ds(g8, SSD_J), :]
        none = jnp.zeros((SSD_N, SSD_N), F32)
        for pr in range(SSD_J // 2):
            pcols = slice(pr * LANES, (pr + 1) * LANES)
            x_p = xs_ref[g, :, pcols]
            s_p = st_ref[g, :, pcols]
            out_rows, upd_rows, ends = [], [], []
            for j in (2 * pr, 2 * pr + 1):
                col = jnp.broadcast_to(a_blk[:, j:j + 1], (q, LANES))
                row = a_t[j:j + 1, :]
                drow = d_t[j:j + 1, :]
                out_rows.append(jnp.where(causal, jnp.exp2(col - row), 0.0) * (cb * drow))
                out_rows.append(jnp.exp2(col) * c_g)
                last = row[:, q - 1:q]
                upd_rows.append(b_t * (drow * jnp.exp2(last - row)))
                upd_rows.append(none)
                ends.append(jnp.exp2(last))
            x_l = jnp.where(left, x_p, 0.0)
            x_r = x_p - x_l
            s_l = jnp.where(left, s_p, 0.0)
            s_r = s_p - s_l
            lhs = jnp.concatenate([jnp.concatenate(out_rows, axis=1),
                                   jnp.concatenate(upd_rows, axis=1)], axis=0).astype(BF16)
            rhs = jnp.concatenate([x_l, s_l, x_r, s_r], axis=0).astype(BF16)
            res = jnp.dot(lhs, rhs, preferred_element_type=F32)
            ya_ref[g, :, pcols] = res[0:q]
            decay = jnp.where(left[0:1, :], ends[0], ends[1])
            st_ref[g, :, pcols] = s_p * decay + res[q:]
        return carry

    lax.fori_loop(0, SSD_G, group_body, 0, unroll=True)

    for g in range(SSD_G):
        cols = slice(g * GW, (g + 1) * GW)
        y = ya_ref[g] + xs_ref[g] * dsk_ref[:, cols]
        zz = z_ref[:, cols]
        y = y * (zz * jax.nn.sigmoid(zz))
        ms = jnp.mean(y * y, axis=-1, keepdims=True)
        y_ref[:, cols] = (y * lax.rsqrt(ms + EPS) * nw_ref[:, cols]).astype(y_ref.dtype)


def _ssd(z, xbc, dt, dt_bias, a_log, d_ssd, ssd_norm, name):
    m, inner = z.shape
    cdim = xbc.shape[1]
    heads = dt_bias.shape[0]
    q = SSD_Q
    dtb = jnp.pad(dt_bias, (0, LANES - heads)).reshape(1, LANES)
    alog = jnp.pad(a_log, (0, LANES - heads)).reshape(1, LANES)
    dsk = jnp.repeat(d_ssd, SSD_P).reshape(1, inner)
    full = lambda r, w: pl.BlockSpec((r, w), lambda c: (0, 0))
    return pl.pallas_call(
        _ssd_kernel,
        grid=(m // q,),
        in_specs=[pl.BlockSpec((q, inner), lambda c: (c, 0)),
                  pl.BlockSpec((q, cdim), lambda c: (c, 0)),
                  pl.BlockSpec((q, LANES), lambda c: (c, 0)),
                  full(1, LANES), full(1, LANES), full(1, inner), full(1, inner)],
        out_specs=pl.BlockSpec((q, inner), lambda c: (c, 0)),
        out_shape=jax.ShapeDtypeStruct((m, inner), BF16),
        scratch_shapes=[pltpu.VMEM((SSD_G, q, GW), F32),
                        pltpu.VMEM((SSD_G, q, SSD_N), F32),
                        pltpu.VMEM((SSD_G, q, SSD_N), F32),
                        pltpu.VMEM((SSD_G, SSD_N, GW), F32),
                        pltpu.VMEM((SSD_G, q, GW), F32),
                        pltpu.VMEM((SSD_G, q, LANES), F32),
                        pltpu.VMEM((LANES, q), F32),
                        pltpu.VMEM((LANES, q), F32)],
        compiler_params=_params(("arbitrary",)),
        name=name,
    )(z, xbc, dt, dtb, alog, dsk, ssd_norm.reshape(1, inner))


def _gelu_tanh(x):
    return 0.5 * x * (1.0 + jnp.tanh(math.sqrt(2.0 / math.pi) * (x + 0.044715 * (x * x * x))))


def _s5_kernel(u_ref, wdup_ref, vtdup_ref, lq_ref, dsk_ref, o_ref,
               t_ref, w_ref, vt_ref, ucat_ref, s_ref, h_ref, yscr_ref):
    nc = ucat_ref.shape[0]
    kq = t_ref.shape[0]
    half = s_ref.shape[1] // 2

    @pl.when((pl.program_id(0) == 0) & (pl.program_id(1) == 0))
    def _():
        for s in range(1, S5_Q):
            t_ref[s * LANES:(s + 1) * LANES, 0:s * LANES] = jnp.zeros((LANES, s * LANES), BF16)

    @pl.when(pl.program_id(1) == 0)
    def _():
        h_ref[...] = jnp.zeros_like(h_ref)
        kv = vtdup_ref.shape[0]
        grp = (lax.broadcasted_iota(jnp.int32, (kv, LANES), 0) // S5_C) % S5_GB
        lane_half = lax.broadcasted_iota(jnp.int32, (kv, LANES), 1) // S5_P
        per_tile = LANES // S5_P
        tiles = S5_GB // per_tile
        for k in range(2 * tiles):
            ri = k // tiles
            keep = grp == (k % tiles) * per_tile + lane_half
            src = slice(ri * LANES, (ri + 1) * LANES)
            dst = slice(k * LANES, (k + 1) * LANES)
            w_ref[:, dst] = jnp.where(keep[:kq], wdup_ref[:, src], 0.0).astype(BF16)
            vt_ref[:, dst] = jnp.where(keep, vtdup_ref[:, src], 0.0).astype(BF16)
        k_all = lax.dot_general(w_ref[kq - LANES:kq, :], vt_ref[0:kq, :], (((1,), (1,)), ((), ())),
                                preferred_element_type=F32).astype(BF16)
        for s in range(S5_Q):
            for t in range(s, S5_Q):
                t_ref[s * LANES:(s + 1) * LANES, t * LANES:(t + 1) * LANES] = (
                    k_all[:, (t - s) * LANES:(t - s + 1) * LANES])

    for s in range(S5_Q):
        ucat_ref[:, s * LANES:(s + 1) * LANES] = u_ref[pl.ds(s, nc, stride=S5_Q), :].astype(BF16)
    ucat = ucat_ref[...]
    s_ref[...] = jnp.dot(ucat, w_ref[...], preferred_element_type=F32)
    l_re = lq_ref[:, :half]
    l_im = lq_ref[:, half:]

    parts = 4
    width = kq // parts
    intra = [jnp.dot(ucat_ref[:, 0:(part + 1) * width], t_ref[0:(part + 1) * width, part * width:(part + 1) * width],
                     preferred_element_type=F32) for part in range(parts)]

    h_re, h_im = h_ref[0:1, :half], h_ref[0:1, half:]
    for c in range(nc):
        srow = s_ref[c:c + 1, :]
        s_ref[c:c + 1, :] = jnp.concatenate([h_re, h_im], axis=1)
        h_re, h_im = (l_re * h_re - l_im * h_im + srow[:, :half],
                      l_re * h_im + l_im * h_re + srow[:, half:])
    h_ref[0:1, :half] = h_re
    h_ref[0:1, half:] = h_im

    hprev = s_ref[...].astype(BF16)
    for part in range(parts):
        c0, c1 = part * width, (part + 1) * width
        y = intra[part] + lax.dot_general(hprev, vt_ref[LANES + c0:LANES + c1, :], (((1,), (1,)), ((), ())),
                                          preferred_element_type=F32)
        for s in range(c0 // LANES, c1 // LANES):
            us = u_ref[pl.ds(s, nc, stride=S5_Q), :]
            yb = y[:, s * LANES - c0:(s + 1) * LANES - c0] + dsk_ref[...] * us
            yscr_ref[pl.ds(s, nc, stride=S5_Q), :] = _gelu_tanh(yb)
    o_ref[...] = yscr_ref[...].astype(o_ref.dtype)


def _s5_matrices(lam_re, lam_im, log_step, b_re, b_im, c_re, c_im):
    g = lam_re.shape[0]
    nb = g // S5_GB
    qn = S5_Q
    step = jnp.exp(log_step)[:, None]
    mag = jnp.exp(lam_re * step)
    ang = lam_im * step
    lb_re, lb_im = mag * jnp.cos(ang), mag * jnp.sin(ang)
    denom = lam_re * lam_re + lam_im * lam_im
    nr, ni = lb_re - 1.0, lb_im
    f_re = (nr * lam_re + ni * lam_im) / denom
    f_im = (ni * lam_re - nr * lam_im) / denom
    bb_re = f_re[..., None] * b_re - f_im[..., None] * b_im
    bb_im = f_re[..., None] * b_im + f_im[..., None] * b_re
    pw_re = [jnp.ones_like(lb_re)]
    pw_im = [jnp.zeros_like(lb_im)]
    for _ in range(qn):
        r, i = pw_re[-1], pw_im[-1]
        pw_re.append(r * lb_re - i * lb_im)
        pw_im.append(r * lb_im + i * lb_re)
    pw_re = jnp.stack(pw_re)
    pw_im = jnp.stack(pw_im)
    def per_block(re, im):
        def rows(v):
            nq = v.shape[0]
            v = v.reshape(nq, nb, S5_GB, S5_C, S5_P).transpose(1, 0, 2, 3, 4)
            return v.reshape(nb, nq * LANES, S5_P)
        re, im = rows(re), rows(im)
        return jnp.concatenate([re, re, im, im], axis=-1)

    bt_re = bb_re.transpose(0, 2, 1)[None]
    bt_im = bb_im.transpose(0, 2, 1)[None]
    ps_re = pw_re[:qn][::-1][:, :, None, :]
    ps_im = pw_im[:qn][::-1][:, :, None, :]
    wdup = per_block(ps_re * bt_re - ps_im * bt_im, ps_re * bt_im + ps_im * bt_re)
    m_re = c_re[None] * pw_re[:, :, None, :] - c_im[None] * pw_im[:, :, None, :]
    m_im = c_re[None] * pw_im[:, :, None, :] + c_im[None] * pw_re[:, :, None, :]
    vtdup = per_block(m_re, -m_im)
    lq = jnp.concatenate([pw_re[qn].reshape(nb, 1, S5_GB * S5_P),
                          pw_im[qn].reshape(nb, 1, S5_GB * S5_P)], axis=-1)
    return wdup, vtdup, lq


def _s5(u, mats, d_s5, name):
    m, width = u.shape
    wdup, vtdup, lq = mats
    nb = width // LANES
    rows = m // S5_ROW_BLOCKS
    nc = rows // S5_Q
    kq = S5_Q * LANES
    kv = vtdup.shape[1]
    ks = lq.shape[2]
    dup = wdup.shape[2]
    return pl.pallas_call(
        _s5_kernel,
        grid=(nb, S5_ROW_BLOCKS),
        in_specs=[pl.BlockSpec((rows, LANES), lambda b, r: (r, b)),
                  pl.BlockSpec((None, kq, dup), lambda b, r: (b, 0, 0)),
                  pl.BlockSpec((None, kv, dup), lambda b, r: (b, 0, 0)),
                  pl.BlockSpec((None, 1, ks), lambda b, r: (b, 0, 0)),
                  pl.BlockSpec((1, LANES), lambda b, r: (0, b))],
        out_specs=pl.BlockSpec((rows, LANES), lambda b, r: (r, b)),
        out_shape=jax.ShapeDtypeStruct((m, width), BF16),
        scratch_shapes=[pltpu.VMEM((kq, kq), BF16),
                        pltpu.VMEM((kq, ks), BF16),
                        pltpu.VMEM((kv, ks), BF16),
                        pltpu.VMEM((nc, kq), BF16),
                        pltpu.VMEM((nc, ks), F32),
                        pltpu.VMEM((8, ks), F32),
                        pltpu.VMEM((rows, LANES), F32)],
        compiler_params=_params(("arbitrary", "arbitrary")),
        name=name,
    )(u, wdup, vtdup, lq, d_s5.reshape(1, width))


def kernel(x, meta_tokens, norm_mix, w_in, conv_w, conv_b, dt_bias, a_log, d_ssd, ssd_norm,
           w_ssd_up, lam_re, lam_im, log_step, b_re, b_im, c_re, c_im, d_s5, w_glu, w_out,
           norm_ffn, w_gate_up, w_down, norm_final):
    batch, seq, d = x.shape
    depth = w_in.shape[0]
    heads = dt_bias.shape[1]
    inner = heads * SSD_P
    cdim = conv_w.shape[2]
    s5w = d_s5.shape[1]
    off_xbc, off_dt = inner, inner + cdim
    off_u = off_dt + heads
    off_g = off_u + s5w
    length = N_META + seq
    lp = -(-length // ROW_ALIGN) * ROW_ALIGN

    pad = lp - length

    w_up_b = w_ssd_up.astype(BF16)
    w_glu_b = w_glu.astype(BF16)
    w_down_b = w_down.astype(BF16)

    outs = []
    for b in range(batch):
        h = jnp.concatenate([jnp.zeros((pad, d), F32), meta_tokens.astype(F32), x[b]], axis=0)
        for l in range(depth):
            wl = w_in[l]
            w_z = wl[:, :off_xbc].astype(BF16)
            w_xbc = wl[:, off_xbc:off_dt].astype(BF16)
            w_dt = jnp.pad(wl[:, off_dt:off_u], ((0, 0), (0, LANES - heads))).astype(BF16)
            w_u = wl[:, off_u:off_g].astype(BF16)
            w_g = wl[:, off_g:].astype(BF16)

            hn = _rmsnorm(h, norm_mix[l], BF16, f"norm_mix{l}")
            z = _mm(hn, w_z, F32, f"proj_z{l}")
            xbc = _mm_conv(hn, w_xbc, conv_w[l], conv_b[l], f"proj_xbc{l}", pad=pad)
            dt = _mm(hn, w_dt, F32, f"proj_dt{l}")
            u = _mm(hn, w_u, F32, f"proj_u{l}")
            gates = _mm(hn, w_g, F32, f"proj_gates{l}", act="sigmoid")

            y_a = _ssd(z, xbc, dt, dt_bias[l], a_log[l], d_ssd[l], ssd_norm[l], f"ssd{l}")
            mats = _s5_matrices(lam_re[l], lam_im[l], log_step[l], b_re[l], b_im[l], c_re[l], c_im[l])
            gy = _s5(u, mats, d_s5[l], f"s5{l}")

            merged = _merge(y_a, gy, w_up_b, w_glu_b, l, gates, f"merge{l}")
            h = _mm_res(merged, w_out, h, f"out_proj{l}", 1376, 512, layer=l)

            hn = _rmsnorm(h, norm_ffn[l], BF16, f"norm_ffn{l}")
            hidden = _swiglu(hn, w_gate_up, l, f"swiglu{l}")
            h = _mm_res(hidden, w_down_b, h, f"down_proj{l}", 688, 256, layer=l)
        outs.append(_rmsnorm(h, norm_final, x.dtype, "norm_final", row0=pad + N_META))
    return jnp.stack(outs, axis=0)
```

```python
import functools
import math

import jax
import jax.numpy as jnp
from jax import lax
from jax.experimental import pallas as pl
from jax.experimental.pallas import tpu as pltpu

F32 = jnp.float32
BF16 = jnp.bfloat16

N_META = 16
EPS = 1e-6
SSD_P = 64
SSD_N = 128
SSD_G = 8
SSD_J = 8
SSD_Q = 128
SSD_CONV = 4
GW = SSD_J * SSD_P
S5_C = 16
S5_P = 64
S5_Q = 16
S5_GB = 8
S5_ROW_BLOCKS = 3
LANES = 128
ROW_ALIGN = S5_Q * 8 * S5_ROW_BLOCKS
VMEM_LIMIT = 56 * 1024 * 1024


def _pick(n, cap, align=16):
    best = None
    for d in range(align, min(n, cap) + 1, align):
        if n % d == 0:
            best = d
    assert best is not None, (n, cap)
    return best


def _params(sem):
    return pltpu.CompilerParams(dimension_semantics=sem, vmem_limit_bytes=VMEM_LIMIT)


def _rmsnorm_kernel(x_ref, w_ref, o_ref):
    x = x_ref[...]
    ms = jnp.mean(x * x, axis=-1, keepdims=True)
    o_ref[...] = (x * lax.rsqrt(ms + EPS) * w_ref[...]).astype(o_ref.dtype)


def _rmsnorm(h, w, out_dtype, name, row0=0):
    m, d = h.shape
    m_out = m - row0
    bm = _pick(math.gcd(m_out, row0) if row0 else m_out, 384)
    first = row0 // bm
    return pl.pallas_call(
        _rmsnorm_kernel,
        grid=(m_out // bm,),
        in_specs=[pl.BlockSpec((bm, d), lambda i: (i + first, 0)),
                  pl.BlockSpec((1, d), lambda i: (0, 0))],
        out_specs=pl.BlockSpec((bm, d), lambda i: (i, 0)),
        out_shape=jax.ShapeDtypeStruct((m_out, d), out_dtype),
        compiler_params=_params(("arbitrary",)),
        name=name,
    )(h, w.reshape(1, d))


def _cast_kernel(x_ref, o_ref):
    o_ref[...] = x_ref[...].astype(o_ref.dtype)


def _cast_bf16(w, name):
    depth, k, n = w.shape
    bk = _pick(k, 256)
    return pl.pallas_call(
        _cast_kernel,
        grid=(depth, k // bk),
        in_specs=[pl.BlockSpec((None, bk, n), lambda l, i: (l, i, 0))],
        out_specs=pl.BlockSpec((None, bk, n), lambda l, i: (l, i, 0)),
        out_shape=jax.ShapeDtypeStruct(w.shape, BF16),
        compiler_params=_params(("arbitrary", "arbitrary")),
        name=name,
    )(w)


def _mm_kernel(x_ref, w_ref, o_ref, *, act):
    acc = jnp.dot(x_ref[...], w_ref[...].astype(BF16), preferred_element_type=F32)
    if act == "sigmoid":
        acc = jax.nn.sigmoid(acc)
    o_ref[...] = acc.astype(o_ref.dtype)


def _w_spec(w, k, bn, layer, col0):
    if layer is None:
        return pl.BlockSpec((k, bn), lambda i, j: (0, j))
    assert col0 % bn == 0, (col0, bn)
    return pl.BlockSpec((None, k, bn), lambda i, j: (layer, 0, col0 // bn + j))


def _mm(x, w, out_dtype, name, act=None, bm_cap=1376, bn_cap=512, layer=None, col0=0, n=None):
    m, k = x.shape
    n = w.shape[1] if layer is None else n
    bm = _pick(m, bm_cap)
    bn = _pick(n, bn_cap, LANES)
    return pl.pallas_call(
        functools.partial(_mm_kernel, act=act),
        grid=(m // bm, n // bn),
        in_specs=[pl.BlockSpec((bm, k), lambda i, j: (i, 0)),
                  _w_spec(w, k, bn, layer, col0)],
        out_specs=pl.BlockSpec((bm, bn), lambda i, j: (i, j)),
        out_shape=jax.ShapeDtypeStruct((m, n), out_dtype),
        compiler_params=_params(("arbitrary", "arbitrary")),
        name=name,
    )(x, w)


def _conv_silu(acc, tail, cw_ref, cb_ref):
    rows8 = lax.broadcasted_iota(jnp.int32, tail.shape, 0)
    w_last = cw_ref[SSD_CONV - 1:SSD_CONV, :]
    conv = cb_ref[...] + w_last * acc
    top = cb_ref[...] + w_last * acc[0:8]
    for k in range(SSD_CONV - 1):
        sh = SSD_CONV - 1 - k
        w_k = cw_ref[k:k + 1, :]
        shifted = pltpu.roll(acc, sh, axis=0)
        conv = conv + w_k * shifted
        top = top + w_k * jnp.where(rows8 < sh, pltpu.roll(tail, sh, axis=0), shifted[0:8])
    return conv * jax.nn.sigmoid(conv), top * jax.nn.sigmoid(top)


def _mm_conv_kernel(x_ref, w_ref, cw_ref, cb_ref, o_ref, tail_ref, *, pad):
    i = pl.program_id(0)
    j = pl.program_id(1)
    bm = o_ref.shape[0]

    @pl.when(i == 0)
    def _():
        tail_ref[j] = jnp.zeros(tail_ref.shape[1:], F32)

    acc = jnp.dot(x_ref[...], w_ref[...].astype(BF16), preferred_element_type=F32)
    full, top = _conv_silu(acc, tail_ref[j], cw_ref, cb_ref)
    o_ref[...] = full
    o_ref[0:8, :] = top
    tail_ref[j] = acc[bm - 8:bm]
    if pad:
        @pl.when(i == 0)
        def _():
            o_ref[0:pad, :] = jnp.zeros((pad, o_ref.shape[1]), F32)


def _mm_conv(x, w, conv_w, conv_b, name, bm_cap=1376, bn_cap=512, layer=None, col0=0, pad=0):
    m, k = x.shape
    n = conv_w.shape[1]
    bm = _pick(m, bm_cap)
    bn = _pick(n, bn_cap, LANES)
    assert 0 <= pad <= bm and pad % 8 == 0, pad
    return pl.pallas_call(
        functools.partial(_mm_conv_kernel, pad=pad),
        grid=(m // bm, n // bn),
        in_specs=[pl.BlockSpec((bm, k), lambda i, j: (i, 0)),
                  _w_spec(w, k, bn, layer, col0),
                  pl.BlockSpec((SSD_CONV, bn), lambda i, j: (0, j)),
                  pl.BlockSpec((1, bn), lambda i, j: (0, j))],
        out_specs=pl.BlockSpec((bm, bn), lambda i, j: (i, j)),
        out_shape=jax.ShapeDtypeStruct((m, n), F32),
        scratch_shapes=[pltpu.VMEM((n // bn, 8, bn), F32)],
        compiler_params=_params(("arbitrary", "arbitrary")),
        name=name,
    )(x, w, conv_w, conv_b.reshape(1, n))


def _mm_res_kernel(x_ref, w_ref, r_ref, o_ref):
    o_ref[...] = r_ref[...] + jnp.dot(x_ref[...], w_ref[...].astype(BF16), preferred_element_type=F32)


def _mm_res(x, w, res, name, bm_cap, bn_cap, layer=None):
    m, k = x.shape
    n = w.shape[-1]
    bm = _pick(m, bm_cap)
    bn = _pick(n, bn_cap, LANES)
    return pl.pallas_call(
        _mm_res_kernel,
        grid=(m // bm, n // bn),
        in_specs=[pl.BlockSpec((bm, k), lambda i, j: (i, 0)),
                  _w_spec(w, k, bn, layer, 0),
                  pl.BlockSpec((bm, bn), lambda i, j: (i, j))],
        out_specs=pl.BlockSpec((bm, bn), lambda i, j: (i, j)),
        out_shape=jax.ShapeDtypeStruct((m, n), F32),
        input_output_aliases={2: 0},
        compiler_params=_params(("arbitrary", "arbitrary")),
        name=name,
    )(x, w, res)


def _swiglu_kernel(x_ref, wg_ref, wu_ref, o_ref):
    x = x_ref[...]
    g = jnp.dot(x, wg_ref[...].astype(BF16), preferred_element_type=F32)
    u = jnp.dot(x, wu_ref[...].astype(BF16), preferred_element_type=F32)
    o_ref[...] = (g * jax.nn.sigmoid(g) * u).astype(o_ref.dtype)


def _swiglu(x, w_gate_up, layer, name):
    m, k = x.shape
    f = w_gate_up.shape[2] // 2
    bm = _pick(m, 1376)
    bn = _pick(f, 512, LANES)
    nb = f // bn
    return pl.pallas_call(
        _swiglu_kernel,
        grid=(m // bm, nb),
        in_specs=[pl.BlockSpec((bm, k), lambda i, j: (i, 0)),
                  pl.BlockSpec((None, k, bn), lambda i, j: (layer, 0, j)),
                  pl.BlockSpec((None, k, bn), lambda i, j: (layer, 0, j + nb))],
        out_specs=pl.BlockSpec((bm, bn), lambda i, j: (i, j)),
        out_shape=jax.ShapeDtypeStruct((m, f), BF16),
        compiler_params=_params(("arbitrary", "arbitrary")),
        name=name,
    )(x, w_gate_up, w_gate_up)


def _merge_kernel(ya_ref, gy_ref, wup_ref, wg1_ref, wg2_ref, ga_ref, gb_ref, o_ref):
    a = jnp.dot(ya_ref[...], wup_ref[...], preferred_element_type=F32)
    gy = gy_ref[...]
    g1 = jnp.dot(gy, wg1_ref[...], preferred_element_type=F32)
    g2 = jnp.dot(gy, wg2_ref[...], preferred_element_type=F32)
    o_ref[...] = (ga_ref[...] * a + gb_ref[...] * (g1 * jax.nn.sigmoid(g2))).astype(o_ref.dtype)


def _merge(ya, gy, w_up, w_glu, layer, gates, name):
    m, d = ya.shape
    k2 = gy.shape[1]
    bm = _pick(m, 688)
    bn = _pick(d, 512, LANES)
    nb = d // bn
    return pl.pallas_call(
        _merge_kernel,
        grid=(m // bm, nb),
        in_specs=[pl.BlockSpec((bm, d), lambda i, j: (i, 0)),
                  pl.BlockSpec((bm, k2), lambda i, j: (i, 0)),
                  pl.BlockSpec((None, d, bn), lambda i, j: (layer, 0, j)),
                  pl.BlockSpec((None, k2, bn), lambda i, j: (layer, 0, j)),
                  pl.BlockSpec((None, k2, bn), lambda i, j: (layer, 0, j + nb)),
                  pl.BlockSpec((bm, bn), lambda i, j: (i, j)),
                  pl.BlockSpec((bm, bn), lambda i, j: (i, j + nb))],
        out_specs=pl.BlockSpec((bm, bn), lambda i, j: (i, j)),
        out_shape=jax.ShapeDtypeStruct((m, d), BF16),
        compiler_params=_params(("arbitrary", "arbitrary")),
        name=name,
    )(ya, gy, w_up, w_glu, w_glu, gates, gates)


def _softplus(x):
    return jnp.maximum(x, 0.0) + jnp.log1p(jnp.exp(-jnp.abs(x)))


def _ssd_kernel(z_ref, xbc_ref, dt_ref, dtb_ref, alog_ref, dsk_ref, nw_ref,
                y_ref,
                xs_ref, bm_ref, cm_ref, st_ref, ya_ref, acsg_ref, acst_ref, dtt_ref):
    q = SSD_Q
    c = pl.program_id(0)

    @pl.when(c == 0)
    def _():
        st_ref[...] = jnp.zeros_like(st_ref)

    n_x = SSD_G * GW
    for g in range(SSD_G):
        xs_ref[g] = xbc_ref[:, g * GW:(g + 1) * GW]
        bm_ref[g] = xbc_ref[:, n_x + g * SSD_N:n_x + (g + 1) * SSD_N]
        cm_ref[g] = xbc_ref[:, n_x + (SSD_G + g) * SSD_N:n_x + (SSD_G + g + 1) * SSD_N]

    dt = _softplus(dt_ref[...] + dtb_ref[...])
    acs = dt * (-jnp.exp(alog_ref[...]))
    rows = lax.broadcasted_iota(jnp.int32, (q, LANES), 0)
    lanes = lax.broadcasted_iota(jnp.int32, (q, LANES), 1)
    sh = 1
    while sh < q:
        acs = acs + jnp.where(rows >= sh, pltpu.roll(acs, sh, axis=0), 0.0)
        sh *= 2
    acs = acs * math.log2(math.e)
    acst_ref[...] = acs.T
    dtt_ref[...] = dt.T
    for g in range(SSD_G):
        acsg_ref[g] = acs if g == 0 else pltpu.roll(acs, LANES - SSD_J * g, axis=1)
    causal = rows >= lanes

    left = lanes < SSD_P

    def group_body(g, carry):
        b_g = bm_ref[g]
        c_g = cm_ref[g]
        cb = lax.dot_general(c_g.astype(BF16), b_g.astype(BF16), (((1,), (1,)), ((), ())),
                             preferred_element_type=F32)
        b_t = b_g.T
        a_blk = acsg_ref[g]
        g8 = pl.multiple_of(g * SSD_J, SSD_J)
        a_t = acst_ref[pl.ds(g8, SSD_J), :]
        d_t = dtt_ref[pl.ds(g8, SSD_J), :]
        none = jnp.zeros((SSD_N, SSD_N), F32)
        for pr in range(SSD_J // 2):
            pcols = slice(pr * LANES, (pr + 1) * LANES)
            x_p = xs_ref[g, :, pcols]
            s_p = st_ref[g, :, pcols]
            out_rows, upd_rows, ends = [], [], []
            for j in (2 * pr, 2 * pr + 1):
                col = jnp.broadcast_to(a_blk[:, j:j + 1], (q, LANES))
                row = a_t[j:j + 1, :]
                drow = d_t[j:j + 1, :]
                out_rows.append(jnp.where(causal, jnp.exp2(col - row), 0.0) * (cb * drow))
                out_rows.append(jnp.exp2(col) * c_g)
                last = row[:, q - 1:q]
                upd_rows.append(b_t * (drow * jnp.exp2(last - row)))
                upd_rows.append(none)
                ends.append(jnp.exp2(last))
            x_l = jnp.where(left, x_p, 0.0)
            x_r = x_p - x_l
            s_l = jnp.where(left, s_p, 0.0)
            s_r = s_p - s_l
            lhs = jnp.concatenate([jnp.concatenate(out_rows, axis=1),
                                   jnp.concatenate(upd_rows, axis=1)], axis=0).astype(BF16)
            rhs = jnp.concatenate([x_l, s_l, x_r, s_r], axis=0).astype(BF16)
            res = jnp.dot(lhs, rhs, preferred_element_type=F32)
            ya_ref[g, :, pcols] = res[0:q]
            decay = jnp.where(left[0:1, :], ends[0], ends[1])
            st_ref[g, :, pcols] = s_p * decay + res[q:]
        return carry

    lax.fori_loop(0, SSD_G, group_body, 0, unroll=True)

    for g in range(SSD_G):
        cols = slice(g * GW, (g + 1) * GW)
        y = ya_ref[g] + xs_ref[g] * dsk_ref[:, cols]
        zz = z_ref[:, cols]
        y = y * (zz * jax.nn.sigmoid(zz))
        ms = jnp.mean(y * y, axis=-1, keepdims=True)
        y_ref[:, cols] = (y * lax.rsqrt(ms + EPS) * nw_ref[:, cols]).astype(y_ref.dtype)


def _ssd(z, xbc, dt, dt_bias, a_log, d_ssd, ssd_norm, name):
    m, inner = z.shape
    cdim = xbc.shape[1]
    heads = dt_bias.shape[0]
    q = SSD_Q
    dtb = jnp.pad(dt_bias, (0, LANES - heads)).reshape(1, LANES)
    alog = jnp.pad(a_log, (0, LANES - heads)).reshape(1, LANES)
    dsk = jnp.repeat(d_ssd, SSD_P).reshape(1, inner)
    full = lambda r, w: pl.BlockSpec((r, w), lambda c: (0, 0))
    return pl.pallas_call(
        _ssd_kernel,
        grid=(m // q,),
        in_specs=[pl.BlockSpec((q, inner), lambda c: (c, 0)),
                  pl.BlockSpec((q, cdim), lambda c: (c, 0)),
                  pl.BlockSpec((q, LANES), lambda c: (c, 0)),
                  full(1, LANES), full(1, LANES), full(1, inner), full(1, inner)],
        out_specs=pl.BlockSpec((q, inner), lambda c: (c, 0)),
        out_shape=jax.ShapeDtypeStruct((m, inner), BF16),
        scratch_shapes=[pltpu.VMEM((SSD_G, q, GW), F32),
                        pltpu.VMEM((SSD_G, q, SSD_N), F32),
                        pltpu.VMEM((SSD_G, q, SSD_N), F32),
                        pltpu.VMEM((SSD_G, SSD_N, GW), F32),
                        pltpu.VMEM((SSD_G, q, GW), F32),
                        pltpu.VMEM((SSD_G, q, LANES), F32),
                        pltpu.VMEM((LANES, q), F32),
                        pltpu.VMEM((LANES, q), F32)],
        compiler_params=_params(("arbitrary",)),
        name=name,
    )(z, xbc, dt, dtb, alog, dsk, ssd_norm.reshape(1, inner))


def _gelu_tanh(x):
    return 0.5 * x * (1.0 + jnp.tanh(math.sqrt(2.0 / math.pi) * (x + 0.044715 * (x * x * x))))


def _s5_kernel(u_ref, wdup_ref, vtdup_ref, lq_ref, dsk_ref, o_ref,
               t_ref, w_ref, vt_ref, ucat_ref, s_ref, h_ref, yscr_ref):
    nc = ucat_ref.shape[0]
    kq = t_ref.shape[0]
    half = s_ref.shape[1] // 2

    @pl.when((pl.program_id(0) == 0) & (pl.program_id(1) == 0))
    def _():
        for s in range(1, S5_Q):
            t_ref[s * LANES:(s + 1) * LANES, 0:s * LANES] = jnp.zeros((LANES, s * LANES), BF16)

    @pl.when(pl.program_id(1) == 0)
    def _():
        h_ref[...] = jnp.zeros_like(h_ref)
        kv = vtdup_ref.shape[0]
        grp = (lax.broadcasted_iota(jnp.int32, (kv, LANES), 0) // S5_C) % S5_GB
        lane_half = lax.broadcasted_iota(jnp.int32, (kv, LANES), 1) // S5_P
        per_tile = LANES // S5_P
        tiles = S5_GB // per_tile
        for k in range(2 * tiles):
            ri = k // tiles
            keep = grp == (k % tiles) * per_tile + lane_half
            src = slice(ri * LANES, (ri + 1) * LANES)
            dst = slice(k * LANES, (k + 1) * LANES)
            w_ref[:, dst] = jnp.where(keep[:kq], wdup_ref[:, src], 0.0).astype(BF16)
            vt_ref[:, dst] = jnp.where(keep, vtdup_ref[:, src], 0.0).astype(BF16)
        k_all = lax.dot_general(w_ref[kq - LANES:kq, :], vt_ref[0:kq, :], (((1,), (1,)), ((), ())),
                                preferred_element_type=F32).astype(BF16)
        for s in range(S5_Q):
            for t in range(s, S5_Q):
                t_ref[s * LANES:(s + 1) * LANES, t * LANES:(t + 1) * LANES] = (
                    k_all[:, (t - s) * LANES:(t - s + 1) * LANES])

    for s in range(S5_Q):
        ucat_ref[:, s * LANES:(s + 1) * LANES] = u_ref[pl.ds(s, nc, stride=S5_Q), :].astype(BF16)
    ucat = ucat_ref[...]
    s_ref[...] = jnp.dot(ucat, w_ref[...], preferred_element_type=F32)
    l_re = lq_ref[:, :half]
    l_im = lq_ref[:, half:]

    parts = 4
    width = kq // parts
    intra = [jnp.dot(ucat_ref[:, 0:(part + 1) * width], t_ref[0:(part + 1) * width, part * width:(part + 1) * width],
                     preferred_element_type=F32) for part in range(parts)]

    h_re, h_im = h_ref[0:1, :half], h_ref[0:1, half:]
    for c in range(nc):
        srow = s_ref[c:c + 1, :]
        s_ref[c:c + 1, :] = jnp.concatenate([h_re, h_im], axis=1)
        h_re, h_im = (l_re * h_re - l_im * h_im + srow[:, :half],
                      l_re * h_im + l_im * h_re + srow[:, half:])
    h_ref[0:1, :half] = h_re
    h_ref[0:1, half:] = h_im

    hprev = s_ref[...].astype(BF16)
    for part in range(parts):
        c0, c1 = part * width, (part + 1) * width
        y = intra[part] + lax.dot_general(hprev, vt_ref[LANES + c0:LANES + c1, :], (((1,), (1,)), ((), ())),
                                          preferred_element_type=F32)
        for s in range(c0 // LANES, c1 // LANES):
            us = u_ref[pl.ds(s, nc, stride=S5_Q), :]
            yb = y[:, s * LANES - c0:(s + 1) * LANES - c0] + dsk_ref[...] * us
            yscr_ref[pl.ds(s, nc, stride=S5_Q), :] = _gelu_tanh(yb)
    o_ref[...] = yscr_ref[...].astype(o_ref.dtype)


def _s5_matrices(lam_re, lam_im, log_step, b_re, b_im, c_re, c_im):
    g = lam_re.shape[0]
    nb = g // S5_GB
    qn = S5_Q
    step = jnp.exp(log_step)[:, None]
    mag = jnp.exp(lam_re * step)
    ang = lam_im * step
    lb_re, lb_im = mag * jnp.cos(ang), mag * jnp.sin(ang)
    denom = lam_re * lam_re + lam_im * lam_im
    nr, ni = lb_re - 1.0, lb_im
    f_re = (nr * lam_re + ni * lam_im) / denom
    f_im = (ni * lam_re - nr * lam_im) / denom
    bb_re = f_re[..., None] * b_re - f_im[..., None] * b_im
    bb_im = f_re[..., None] * b_im + f_im[..., None] * b_re
    pw_re = [jnp.ones_like(lb_re)]
    pw_im = [jnp.zeros_like(lb_im)]
    for _ in range(qn):
        r, i = pw_re[-1], pw_im[-1]
        pw_re.append(r * lb_re - i * lb_im)
        pw_im.append(r * lb_im + i * lb_re)
    pw_re = jnp.stack(pw_re)
    pw_im = jnp.stack(pw_im)
    def per_block(re, im):
        def rows(v):
            nq = v.shape[0]
            v = v.reshape(nq, nb, S5_GB, S5_C, S5_P).transpose(1, 0, 2, 3, 4)
            return v.reshape(nb, nq * LANES, S5_P)
        re, im = rows(re), rows(im)
        return jnp.concatenate([re, re, im, im], axis=-1)

    bt_re = bb_re.transpose(0, 2, 1)[None]
    bt_im = bb_im.transpose(0, 2, 1)[None]
    ps_re = pw_re[:qn][::-1][:, :, None, :]
    ps_im = pw_im[:qn][::-1][:, :, None, :]
    wdup = per_block(ps_re * bt_re - ps_im * bt_im, ps_re * bt_im + ps_im * bt_re)
    m_re = c_re[None] * pw_re[:, :, None, :] - c_im[None] * pw_im[:, :, None, :]
    m_im = c_re[None] * pw_im[:, :, None, :] + c_im[None] * pw_re[:, :, None, :]
    vtdup = per_block(m_re, -m_im)
    lq = jnp.concatenate([pw_re[qn].reshape(nb, 1, S5_GB * S5_P),
                          pw_im[qn].reshape(nb, 1, S5_GB * S5_P)], axis=-1)
    return wdup, vtdup, lq


def _s5(u, mats, d_s5, name):
    m, width = u.shape
    wdup, vtdup, lq = mats
    nb = width // LANES
    rows = m // S5_ROW_BLOCKS
    nc = rows // S5_Q
    kq = S5_Q * LANES
    kv = vtdup.shape[1]
    ks = lq.shape[2]
    dup = wdup.shape[2]
    return pl.pallas_call(
        _s5_kernel,
        grid=(nb, S5_ROW_BLOCKS),
        in_specs=[pl.BlockSpec((rows, LANES), lambda b, r: (r, b)),
                  pl.BlockSpec((None, kq, dup), lambda b, r: (b, 0, 0)),
                  pl.BlockSpec((None, kv, dup), lambda b, r: (b, 0, 0)),
                  pl.BlockSpec((None, 1, ks), lambda b, r: (b, 0, 0)),
                  pl.BlockSpec((1, LANES), lambda b, r: (0, b))],
        out_specs=pl.BlockSpec((rows, LANES), lambda b, r: (r, b)),
        out_shape=jax.ShapeDtypeStruct((m, width), BF16),
        scratch_shapes=[pltpu.VMEM((kq, kq), BF16),
                        pltpu.VMEM((kq, ks), BF16),
                        pltpu.VMEM((kv, ks), BF16),
                        pltpu.VMEM((nc, kq), BF16),
                        pltpu.VMEM((nc, ks), F32),
                        pltpu.VMEM((8, ks), F32),
                        pltpu.VMEM((rows, LANES), F32)],
        compiler_params=_params(("arbitrary", "arbitrary")),
        name=name,
    )(u, wdup, vtdup, lq, d_s5.reshape(1, width))


def kernel(x, meta_tokens, norm_mix, w_in, conv_w, conv_b, dt_bias, a_log, d_ssd, ssd_norm,
           w_ssd_up, lam_re, lam_im, log_step, b_re, b_im, c_re, c_im, d_s5, w_glu, w_out,
           norm_ffn, w_gate_up, w_down, norm_final):
    batch, seq, d = x.shape
    depth = w_in.shape[0]
    heads = dt_bias.shape[1]
    inner = heads * SSD_P
    cdim = conv_w.shape[2]
    s5w = d_s5.shape[1]
    off_xbc, off_dt = inner, inner + cdim
    off_u = off_dt + heads
    off_g = off_u + s5w
    length = N_META + seq
    lp = -(-length // ROW_ALIGN) * ROW_ALIGN

    pad = lp - length

    w_up_b = _cast_bf16(w_ssd_up, "cast_w_ssd_up")
    w_glu_b = _cast_bf16(w_glu, "cast_w_glu")
    w_down_b = w_down.astype(BF16)

    outs = []
    for b in range(batch):
        h = jnp.concatenate([jnp.zeros((pad, d), F32), meta_tokens.astype(F32), x[b]], axis=0)
        for l in range(depth):
            wl = w_in[l]
            w_z = wl[:, :off_xbc].astype(BF16)
            w_xbc = wl[:, off_xbc:off_dt].astype(BF16)
            w_dt = jnp.pad(wl[:, off_dt:off_u], ((0, 0), (0, LANES - heads))).astype(BF16)
            w_u = wl[:, off_u:off_g].astype(BF16)
            w_g = wl[:, off_g:].astype(BF16)

            hn = _rmsnorm(h, norm_mix[l], BF16, f"norm_mix{l}")
            z = _mm(hn, w_z, F32, f"proj_z{l}")
            xbc = _mm_conv(hn, w_xbc, conv_w[l], conv_b[l], f"proj_xbc{l}", pad=pad)
            dt = _mm(hn, w_dt, F32, f"proj_dt{l}")
            u = _mm(hn, w_u, F32, f"proj_u{l}")
            gates = _mm(hn, w_g, F32, f"proj_gates{l}", act="sigmoid")

            y_a = _ssd(z, xbc, dt, dt_bias[l], a_log[l], d_ssd[l], ssd_norm[l], f"ssd{l}")
            mats = _s5_matrices(lam_re[l], lam_im[l], log_step[l], b_re[l], b_im[l], c_re[l], c_im[l])
            gy = _s5(u, mats, d_s5[l], f"s5{l}")

            merged = _merge(y_a, gy, w_up_b, w_glu_b, l, gates, f"merge{l}")
            h = _mm_res(merged, w_out, h, f"out_proj{l}", 1376, 512, layer=l)

            hn = _rmsnorm(h, norm_ffn[l], BF16, f"norm_ffn{l}")
            hidden = _swiglu(hn, w_gate_up, l, f"swiglu{l}")
            h = _mm_res(hidden, w_down_b, h, f"down_proj{l}", 688, 256, layer=l)
        outs.append(_rmsnorm(h, norm_final, x.dtype, "norm_final", row0=pad + N_META))
    return jnp.stack(outs, axis=0)
```

```python
import functools
import math

import jax
import jax.numpy as jnp
from jax import lax
from jax.experimental import pallas as pl
from jax.experimental.pallas import tpu as pltpu

F32 = jnp.float32
BF16 = jnp.bfloat16

N_META = 16
EPS = 1e-6
SSD_P = 64
SSD_N = 128
SSD_G = 8
SSD_J = 8
SSD_Q = 128
SSD_CONV = 4
GW = SSD_J * SSD_P
S5_C = 16
S5_P = 64
S5_Q = 16
S5_GB = 8
S5_ROW_BLOCKS = 3
LANES = 128
ROW_ALIGN = S5_Q * 8 * S5_ROW_BLOCKS
VMEM_LIMIT = 56 * 1024 * 1024


def _pick(n, cap, align=16):
    best = None
    for d in range(align, min(n, cap) + 1, align):
        if n % d == 0:
            best = d
    assert best is not None, (n, cap)
    return best


def _params(sem):
    return pltpu.CompilerParams(dimension_semantics=sem, vmem_limit_bytes=VMEM_LIMIT)


def _rmsnorm_kernel(x_ref, w_ref, o_ref):
    x = x_ref[...]
    ms = jnp.mean(x * x, axis=-1, keepdims=True)
    o_ref[...] = (x * lax.rsqrt(ms + EPS) * w_ref[...]).astype(o_ref.dtype)


def _rmsnorm(h, w, out_dtype, name, row0=0):
    m, d = h.shape
    m_out = m - row0
    bm = _pick(math.gcd(m_out, row0) if row0 else m_out, 384)
    first = row0 // bm
    return pl.pallas_call(
        _rmsnorm_kernel,
        grid=(m_out // bm,),
        in_specs=[pl.BlockSpec((bm, d), lambda i: (i + first, 0)),
                  pl.BlockSpec((1, d), lambda i: (0, 0))],
        out_specs=pl.BlockSpec((bm, d), lambda i: (i, 0)),
        out_shape=jax.ShapeDtypeStruct((m_out, d), out_dtype),
        compiler_params=_params(("arbitrary",)),
        name=name,
    )(h, w.reshape(1, d))


def _mm_kernel(x_ref, w_ref, o_ref, *, act):
    acc = jnp.dot(x_ref[...], w_ref[...].astype(BF16), preferred_element_type=F32)
    if act == "sigmoid":
        acc = jax.nn.sigmoid(acc)
    o_ref[...] = acc.astype(o_ref.dtype)


def _w_spec(w, k, bn, layer, col0):
    if layer is None:
        return pl.BlockSpec((k, bn), lambda i, j: (0, j))
    assert col0 % bn == 0, (col0, bn)
    return pl.BlockSpec((None, k, bn), lambda i, j: (layer, 0, col0 // bn + j))


def _mm(x, w, out_dtype, name, act=None, bm_cap=1376, bn_cap=512, layer=None, col0=0, n=None):
    m, k = x.shape
    n = w.shape[1] if layer is None else n
    bm = _pick(m, bm_cap)
    bn = _pick(n, bn_cap, LANES)
    return pl.pallas_call(
        functools.partial(_mm_kernel, act=act),
        grid=(m // bm, n // bn),
        in_specs=[pl.BlockSpec((bm, k), lambda i, j: (i, 0)),
                  _w_spec(w, k, bn, layer, col0)],
        out_specs=pl.BlockSpec((bm, bn), lambda i, j: (i, j)),
        out_shape=jax.ShapeDtypeStruct((m, n), out_dtype),
        compiler_params=_params(("arbitrary", "arbitrary")),
        name=name,
    )(x, w)


def _conv_silu(acc, tail, cw_ref, cb_ref):
    rows8 = lax.broadcasted_iota(jnp.int32, tail.shape, 0)
    w_last = cw_ref[SSD_CONV - 1:SSD_CONV, :]
    conv = cb_ref[...] + w_last * acc
    top = cb_ref[...] + w_last * acc[0:8]
    for k in range(SSD_CONV - 1):
        sh = SSD_CONV - 1 - k
        w_k = cw_ref[k:k + 1, :]
        shifted = pltpu.roll(acc, sh, axis=0)
        conv = conv + w_k * shifted
        top = top + w_k * jnp.where(rows8 < sh, pltpu.roll(tail, sh, axis=0), shifted[0:8])
    return conv * jax.nn.sigmoid(conv), top * jax.nn.sigmoid(top)


def _mm_conv_kernel(x_ref, w_ref, cw_ref, cb_ref, o_ref, tail_ref, *, pad):
    i = pl.program_id(0)
    j = pl.program_id(1)
    bm = o_ref.shape[0]

    @pl.when(i == 0)
    def _():
        tail_ref[j] = jnp.zeros(tail_ref.shape[1:], F32)

    acc = jnp.dot(x_ref[...], w_ref[...].astype(BF16), preferred_element_type=F32)
    full, top = _conv_silu(acc, tail_ref[j], cw_ref, cb_ref)
    o_ref[...] = full
    o_ref[0:8, :] = top
    tail_ref[j] = acc[bm - 8:bm]
    if pad:
        @pl.when(i == 0)
        def _():
            o_ref[0:pad, :] = jnp.zeros((pad, o_ref.shape[1]), F32)


def _mm_conv(x, w, conv_w, conv_b, name, bm_cap=1376, bn_cap=512, layer=None, col0=0, pad=0):
    m, k = x.shape
    n = conv_w.shape[1]
    bm = _pick(m, bm_cap)
    bn = _pick(n, bn_cap, LANES)
    assert 0 <= pad <= bm and pad % 8 == 0, pad
    return pl.pallas_call(
        functools.partial(_mm_conv_kernel, pad=pad),
        grid=(m // bm, n // bn),
        in_specs=[pl.BlockSpec((bm, k), lambda i, j: (i, 0)),
                  _w_spec(w, k, bn, layer, col0),
                  pl.BlockSpec((SSD_CONV, bn), lambda i, j: (0, j)),
                  pl.BlockSpec((1, bn), lambda i, j: (0, j))],
        out_specs=pl.BlockSpec((bm, bn), lambda i, j: (i, j)),
        out_shape=jax.ShapeDtypeStruct((m, n), F32),
        scratch_shapes=[pltpu.VMEM((n // bn, 8, bn), F32)],
        compiler_params=_params(("arbitrary", "arbitrary")),
        name=name,
    )(x, w, conv_w, conv_b.reshape(1, n))


def _mm_res_kernel(x_ref, w_ref, r_ref, o_ref):
    o_ref[...] = r_ref[...] + jnp.dot(x_ref[...], w_ref[...].astype(BF16), preferred_element_type=F32)


def _mm_res(x, w, res, name, bm_cap, bn_cap, layer=None):
    m, k = x.shape
    n = w.shape[-1]
    bm = _pick(m, bm_cap)
    bn = _pick(n, bn_cap, LANES)
    return pl.pallas_call(
        _mm_res_kernel,
        grid=(m // bm, n // bn),
        in_specs=[pl.BlockSpec((bm, k), lambda i, j: (i, 0)),
                  _w_spec(w, k, bn, layer, 0),
                  pl.BlockSpec((bm, bn), lambda i, j: (i, j))],
        out_specs=pl.BlockSpec((bm, bn), lambda i, j: (i, j)),
        out_shape=jax.ShapeDtypeStruct((m, n), F32),
        input_output_aliases={2: 0},
        compiler_params=_params(("arbitrary", "arbitrary")),
        name=name,
    )(x, w, res)


def _swiglu_kernel(x_ref, wg_ref, wu_ref, o_ref):
    x = x_ref[...]
    g = jnp.dot(x, wg_ref[...].astype(BF16), preferred_element_type=F32)
    u = jnp.dot(x, wu_ref[...].astype(BF16), preferred_element_type=F32)
    o_ref[...] = (g * jax.nn.sigmoid(g) * u).astype(o_ref.dtype)


def _swiglu(x, w_gate_up, layer, name):
    m, k = x.shape
    f = w_gate_up.shape[2] // 2
    bm = _pick(m, 1376)
    bn = _pick(f, 512, LANES)
    nb = f // bn
    return pl.pallas_call(
        _swiglu_kernel,
        grid=(m // bm, nb),
        in_specs=[pl.BlockSpec((bm, k), lambda i, j: (i, 0)),
                  pl.BlockSpec((None, k, bn), lambda i, j: (layer, 0, j)),
                  pl.BlockSpec((None, k, bn), lambda i, j: (layer, 0, j + nb))],
        out_specs=pl.BlockSpec((bm, bn), lambda i, j: (i, j)),
        out_shape=jax.ShapeDtypeStruct((m, f), BF16),
        compiler_params=_params(("arbitrary", "arbitrary")),
        name=name,
    )(x, w_gate_up, w_gate_up)


def _merge_kernel(ya_ref, gy_ref, wup_ref, wg1_ref, wg2_ref, ga_ref, gb_ref, o_ref):
    a = jnp.dot(ya_ref[...], wup_ref[...].astype(BF16), preferred_element_type=F32)
    gy = gy_ref[...]
    g1 = jnp.dot(gy, wg1_ref[...].astype(BF16), preferred_element_type=F32)
    g2 = jnp.dot(gy, wg2_ref[...].astype(BF16), preferred_element_type=F32)
    o_ref[...] = (ga_ref[...] * a + gb_ref[...] * (g1 * jax.nn.sigmoid(g2))).astype(o_ref.dtype)


def _merge(ya, gy, w_up, w_glu, layer, gates, name):
    m, d = ya.shape
    k2 = gy.shape[1]
    bm = _pick(m, 1376)
    bn = _pick(d, 256, LANES)
    nb = d // bn
    once = pl.Buffered(1)
    return pl.pallas_call(
        _merge_kernel,
        grid=(m // bm, nb),
        in_specs=[pl.BlockSpec((bm, d), lambda i, j: (i, 0), pipeline_mode=once),
                  pl.BlockSpec((bm, k2), lambda i, j: (i, 0), pipeline_mode=once),
                  pl.BlockSpec((None, d, bn), lambda i, j: (layer, 0, j)),
                  pl.BlockSpec((None, k2, bn), lambda i, j: (layer, 0, j)),
                  pl.BlockSpec((None, k2, bn), lambda i, j: (layer, 0, j + nb)),
                  pl.BlockSpec((bm, bn), lambda i, j: (i, j)),
                  pl.BlockSpec((bm, bn), lambda i, j: (i, j + nb))],
        out_specs=pl.BlockSpec((bm, bn), lambda i, j: (i, j)),
        out_shape=jax.ShapeDtypeStruct((m, d), BF16),
        compiler_params=_params(("arbitrary", "arbitrary")),
        name=name,
    )(ya, gy, w_up, w_glu, w_glu, gates, gates)


def _softplus(x):
    return jnp.maximum(x, 0.0) + jnp.log1p(jnp.exp(-jnp.abs(x)))


def _ssd_kernel(z_ref, xbc_ref, dt_ref, dtb_ref, alog_ref, dsk_ref, nw_ref,
                y_ref,
                xs_ref, bm_ref, cm_ref, st_ref, ya_ref, acsg_ref, acst_ref, dtt_ref):
    q = SSD_Q
    c = pl.program_id(0)

    @pl.when(c == 0)
    def _():
        st_ref[...] = jnp.zeros_like(st_ref)

    n_x = SSD_G * GW
    for g in range(SSD_G):
        xs_ref[g] = xbc_ref[:, g * GW:(g + 1) * GW]
        bm_ref[g] = xbc_ref[:, n_x + g * SSD_N:n_x + (g + 1) * SSD_N]
        cm_ref[g] = xbc_ref[:, n_x + (SSD_G + g) * SSD_N:n_x + (SSD_G + g + 1) * SSD_N]

    dt = _softplus(dt_ref[...] + dtb_ref[...])
    acs = dt * (-jnp.exp(alog_ref[...]))
    rows = lax.broadcasted_iota(jnp.int32, (q, LANES), 0)
    lanes = lax.broadcasted_iota(jnp.int32, (q, LANES), 1)
    sh = 1
    while sh < q:
        acs = acs + jnp.where(rows >= sh, pltpu.roll(acs, sh, axis=0), 0.0)
        sh *= 2
    acs = acs * math.log2(math.e)
    acst_ref[...] = acs.T
    dtt_ref[...] = dt.T
    for g in range(SSD_G):
        acsg_ref[g] = acs if g == 0 else pltpu.roll(acs, LANES - SSD_J * g, axis=1)
    causal = rows >= lanes

    left = lanes < SSD_P

    def group_body(g, carry):
        b_g = bm_ref[g]
        c_g = cm_ref[g]
        cb = lax.dot_general(c_g.astype(BF16), b_g.astype(BF16), (((1,), (1,)), ((), ())),
                             preferred_element_type=F32)
        b_t = b_g.T
        a_blk = acsg_ref[g]
        g8 = pl.multiple_of(g * SSD_J, SSD_J)
        a_t = acst_ref[pl.ds(g8, SSD_J), :]
        d_t = dtt_ref[pl.ds(g8, SSD_J), :]
        none = jnp.zeros((SSD_N, SSD_N), F32)
        for pr in range(SSD_J // 2):
            pcols = slice(pr * LANES, (pr + 1) * LANES)
            x_p = xs_ref[g, :, pcols]
            s_p = st_ref[g, :, pcols]
            out_rows, upd_rows, ends = [], [], []
            for j in (2 * pr, 2 * pr + 1):
                col = jnp.broadcast_to(a_blk[:, j:j + 1], (q, LANES))
                row = a_t[j:j + 1, :]
                drow = d_t[j:j + 1, :]
                out_rows.append(jnp.where(causal, jnp.exp2(col - row), 0.0) * (cb * drow))
                out_rows.append(jnp.exp2(col) * c_g)
                last = row[:, q - 1:q]
                upd_rows.append(b_t * (drow * jnp.exp2(last - row)))
                upd_rows.append(none)
                ends.append(jnp.exp2(last))
            x_l = jnp.where(left, x_p, 0.0)
            x_r = x_p - x_l
            s_l = jnp.where(left, s_p, 0.0)
            s_r = s_p - s_l
            lhs = jnp.concatenate([jnp.concatenate(out_rows, axis=1),
                                   jnp.concatenate(upd_rows, axis=1)], axis=0).astype(BF16)
            rhs = jnp.concatenate([x_l, s_l, x_r, s_r], axis=0).astype(BF16)
            res = jnp.dot(lhs, rhs, preferred_element_type=F32)
            ya_ref[g, :, pcols] = res[0:q]
            decay = jnp.where(left[0:1, :], ends[0], ends[1])
            st_ref[g, :, pcols] = s_p * decay + res[q:]
        return carry

    lax.fori_loop(0, SSD_G, group_body, 0, unroll=True)

    for g in range(SSD_G):
        cols = slice(g * GW, (g + 1) * GW)
        y = ya_ref[g] + xs_ref[g] * dsk_ref[:, cols]
        zz = z_ref[:, cols]
        y = y * (zz * jax.nn.sigmoid(zz))
        ms = jnp.mean(y * y, axis=-1, keepdims=True)
        y_ref[:, cols] = (y * lax.rsqrt(ms + EPS) * nw_ref[:, cols]).astype(y_ref.dtype)


def _ssd(z, xbc, dt, dt_bias, a_log, d_ssd, ssd_norm, name):
    m, inner = z.shape
    cdim = xbc.shape[1]
    heads = dt_bias.shape[0]
    q = SSD_Q
    dtb = jnp.pad(dt_bias, (0, LANES - heads)).reshape(1, LANES)
    alog = jnp.pad(a_log, (0, LANES - heads)).reshape(1, LANES)
    dsk = jnp.repeat(d_ssd, SSD_P).reshape(1, inner)
    full = lambda r, w: pl.BlockSpec((r, w), lambda c: (0, 0))
    return pl.pallas_call(
        _ssd_kernel,
        grid=(m // q,),
        in_specs=[pl.BlockSpec((q, inner), lambda c: (c, 0)),
                  pl.BlockSpec((q, cdim), lambda c: (c, 0)),
                  pl.BlockSpec((q, LANES), lambda c: (c, 0)),
                  full(1, LANES), full(1, LANES), full(1, inner), full(1, inner)],
        out_specs=pl.BlockSpec((q, inner), lambda c: (c, 0)),
        out_shape=jax.ShapeDtypeStruct((m, inner), BF16),
        scratch_shapes=[pltpu.VMEM((SSD_G, q, GW), F32),
                        pltpu.VMEM((SSD_G, q, SSD_N), F32),
                        pltpu.VMEM((SSD_G, q, SSD_N), F32),
                        pltpu.VMEM((SSD_G, SSD_N, GW), F32),
                        pltpu.VMEM((SSD_G, q, GW), F32),
                        pltpu.VMEM((SSD_G, q, LANES), F32),
                        pltpu.VMEM((LANES, q), F32),
                        pltpu.VMEM((LANES, q), F32)],
        compiler_params=_params(("arbitrary",)),
        name=name,
    )(z, xbc, dt, dtb, alog, dsk, ssd_norm.reshape(1, inner))


def _gelu_tanh(x):
    return 0.5 * x * (1.0 + jnp.tanh(math.sqrt(2.0 / math.pi) * (x + 0.044715 * (x * x * x))))


def _s5_kernel(u_ref, wdup_ref, vtdup_ref, lq_ref, dsk_ref, o_ref,
               t_ref, w_ref, vt_ref, ucat_ref, s_ref, h_ref, yscr_ref):
    nc = ucat_ref.shape[0]
    kq = t_ref.shape[0]
    half = s_ref.shape[1] // 2

    @pl.when((pl.program_id(0) == 0) & (pl.program_id(1) == 0))
    def _():
        for s in range(1, S5_Q):
            t_ref[s * LANES:(s + 1) * LANES, 0:s * LANES] = jnp.zeros((LANES, s * LANES), BF16)

    @pl.when(pl.program_id(1) == 0)
    def _():
        h_ref[...] = jnp.zeros_like(h_ref)
        kv = vtdup_ref.shape[0]
        grp = (lax.broadcasted_iota(jnp.int32, (kv, LANES), 0) // S5_C) % S5_GB
        lane_half = lax.broadcasted_iota(jnp.int32, (kv, LANES), 1) // S5_P
        per_tile = LANES // S5_P
        tiles = S5_GB // per_tile
        for k in range(2 * tiles):
            ri = k // tiles
            keep = grp == (k % tiles) * per_tile + lane_half
            src = slice(ri * LANES, (ri + 1) * LANES)
            dst = slice(k * LANES, (k + 1) * LANES)
            w_ref[:, dst] = jnp.where(keep[:kq], wdup_ref[:, src], 0.0).astype(BF16)
            vt_ref[:, dst] = jnp.where(keep, vtdup_ref[:, src], 0.0).astype(BF16)
        k_all = lax.dot_general(w_ref[kq - LANES:kq, :], vt_ref[0:kq, :], (((1,), (1,)), ((), ())),
                                preferred_element_type=F32).astype(BF16)
        for s in range(S5_Q):
            for t in range(s, S5_Q):
                t_ref[s * LANES:(s + 1) * LANES, t * LANES:(t + 1) * LANES] = (
                    k_all[:, (t - s) * LANES:(t - s + 1) * LANES])

    for s in range(S5_Q):
        ucat_ref[:, s * LANES:(s + 1) * LANES] = u_ref[pl.ds(s, nc, stride=S5_Q), :].astype(BF16)
    ucat = ucat_ref[...]
    s_ref[...] = jnp.dot(ucat, w_ref[...], preferred_element_type=F32)
    l_re = lq_ref[:, :half]
    l_im = lq_ref[:, half:]

    parts = 4
    width = kq // parts
    intra = [jnp.dot(ucat_ref[:, 0:(part + 1) * width], t_ref[0:(part + 1) * width, part * width:(part + 1) * width],
                     preferred_element_type=F32) for part in range(parts)]

    h_re, h_im = h_ref[0:1, :half], h_ref[0:1, half:]
    for c in range(nc):
        srow = s_ref[c:c + 1, :]
        s_ref[c:c + 1, :] = jnp.concatenate([h_re, h_im], axis=1)
        h_re, h_im = (l_re * h_re - l_im * h_im + srow[:, :half],
                      l_re * h_im + l_im * h_re + srow[:, half:])
    h_ref[0:1, :half] = h_re
    h_ref[0:1, half:] = h_im

    hprev = s_ref[...].astype(BF16)
    for part in range(parts):
        c0, c1 = part * width, (part + 1) * width
        y = intra[part] + lax.dot_general(hprev, vt_ref[LANES + c0:LANES + c1, :], (((1,), (1,)), ((), ())),
                                          preferred_element_type=F32)
        for s in range(c0 // LANES, c1 // LANES):
            us = u_ref[pl.ds(s, nc, stride=S5_Q), :]
            yb = y[:, s * LANES - c0:(s + 1) * LANES - c0] + dsk_ref[...] * us
            yscr_ref[pl.ds(s, nc, stride=S5_Q), :] = _gelu_tanh(yb)
    o_ref[...] = yscr_ref[...].astype(o_ref.dtype)


def _s5_matrices(lam_re, lam_im, log_step, b_re, b_im, c_re, c_im):
    g = lam_re.shape[0]
    nb = g // S5_GB
    qn = S5_Q
    step = jnp.exp(log_step)[:, None]
    mag = jnp.exp(lam_re * step)
    ang = lam_im * step
    lb_re, lb_im = mag * jnp.cos(ang), mag * jnp.sin(ang)
    denom = lam_re * lam_re + lam_im * lam_im
    nr, ni = lb_re - 1.0, lb_im
    f_re = (nr * lam_re + ni * lam_im) / denom
    f_im = (ni * lam_re - nr * lam_im) / denom
    bb_re = f_re[..., None] * b_re - f_im[..., None] * b_im
    bb_im = f_re[..., None] * b_im + f_im[..., None] * b_re
    pw_re = [jnp.ones_like(lb_re)]
    pw_im = [jnp.zeros_like(lb_im)]
    for _ in range(qn):
        r, i = pw_re[-1], pw_im[-1]
        pw_re.append(r * lb_re - i * lb_im)
        pw_im.append(r * lb_im + i * lb_re)
    pw_re = jnp.stack(pw_re)
    pw_im = jnp.stack(pw_im)
    def per_block(re, im):
        def rows(v):
            nq = v.shape[0]
            v = v.reshape(nq, nb, S5_GB, S5_C, S5_P).transpose(1, 0, 2, 3, 4)
            return v.reshape(nb, nq * LANES, S5_P)
        re, im = rows(re), rows(im)
        return jnp.concatenate([re, re, im, im], axis=-1)

    bt_re = bb_re.transpose(0, 2, 1)[None]
    bt_im = bb_im.transpose(0, 2, 1)[None]
    ps_re = pw_re[:qn][::-1][:, :, None, :]
    ps_im = pw_im[:qn][::-1][:, :, None, :]
    wdup = per_block(ps_re * bt_re - ps_im * bt_im, ps_re * bt_im + ps_im * bt_re)
    m_re = c_re[None] * pw_re[:, :, None, :] - c_im[None] * pw_im[:, :, None, :]
    m_im = c_re[None] * pw_im[:, :, None, :] + c_im[None] * pw_re[:, :, None, :]
    vtdup = per_block(m_re, -m_im)
    lq = jnp.concatenate([pw_re[qn].reshape(nb, 1, S5_GB * S5_P),
                          pw_im[qn].reshape(nb, 1, S5_GB * S5_P)], axis=-1)
    return wdup, vtdup, lq


def _s5(u, mats, d_s5, name):
    m, width = u.shape
    wdup, vtdup, lq = mats
    nb = width // LANES
    rows = m // S5_ROW_BLOCKS
    nc = rows // S5_Q
    kq = S5_Q * LANES
    kv = vtdup.shape[1]
    ks = lq.shape[2]
    dup = wdup.shape[2]
    return pl.pallas_call(
        _s5_kernel,
        grid=(nb, S5_ROW_BLOCKS),
        in_specs=[pl.BlockSpec((rows, LANES), lambda b, r: (r, b)),
                  pl.BlockSpec((None, kq, dup), lambda b, r: (b, 0, 0)),
                  pl.BlockSpec((None, kv, dup), lambda b, r: (b, 0, 0)),
                  pl.BlockSpec((None, 1, ks), lambda b, r: (b, 0, 0)),
                  pl.BlockSpec((1, LANES), lambda b, r: (0, b))],
        out_specs=pl.BlockSpec((rows, LANES), lambda b, r: (r, b)),
        out_shape=jax.ShapeDtypeStruct((m, width), BF16),
        scratch_shapes=[pltpu.VMEM((kq, kq), BF16),
                        pltpu.VMEM((kq, ks), BF16),
                        pltpu.VMEM((kv, ks), BF16),
                        pltpu.VMEM((nc, kq), BF16),
                        pltpu.VMEM((nc, ks), F32),
                        pltpu.VMEM((8, ks), F32),
                        pltpu.VMEM((rows, LANES), F32)],
        compiler_params=_params(("arbitrary", "arbitrary")),
        name=name,
    )(u, wdup, vtdup, lq, d_s5.reshape(1, width))


def kernel(x, meta_tokens, norm_mix, w_in, conv_w, conv_b, dt_bias, a_log, d_ssd, ssd_norm,
           w_ssd_up, lam_re, lam_im, log_step, b_re, b_im, c_re, c_im, d_s5, w_glu, w_out,
           norm_ffn, w_gate_up, w_down, norm_final):
    batch, seq, d = x.shape
    depth = w_in.shape[0]
    heads = dt_bias.shape[1]
    inner = heads * SSD_P
    cdim = conv_w.shape[2]
    s5w = d_s5.shape[1]
    off_xbc, off_dt = inner, inner + cdim
    off_u = off_dt + heads
    off_g = off_u + s5w
    length = N_META + seq
    lp = -(-length // ROW_ALIGN) * ROW_ALIGN

    pad = lp - length

    w_down_b = w_down.astype(BF16)

    outs = []
    for b in range(batch):
        h = jnp.concatenate([jnp.zeros((pad, d), F32), meta_tokens.astype(F32), x[b]], axis=0)
        for l in range(depth):
            wl = w_in[l]
            w_z = wl[:, :off_xbc].astype(BF16)
            w_xbc = wl[:, off_xbc:off_dt].astype(BF16)
            w_dt = jnp.pad(wl[:, off_dt:off_u], ((0, 0), (0, LANES - heads))).astype(BF16)
            w_u = wl[:, off_u:off_g].astype(BF16)
            w_g = wl[:, off_g:].astype(BF16)

            hn = _rmsnorm(h, norm_mix[l], BF16, f"norm_mix{l}")
            z = _mm(hn, w_z, F32, f"proj_z{l}")
            xbc = _mm_conv(hn, w_xbc, conv_w[l], conv_b[l], f"proj_xbc{l}", pad=pad)
            dt = _mm(hn, w_dt, F32, f"proj_dt{l}")
            u = _mm(hn, w_u, F32, f"proj_u{l}")
            gates = _mm(hn, w_g, F32, f"proj_gates{l}", act="sigmoid")

            y_a = _ssd(z, xbc, dt, dt_bias[l], a_log[l], d_ssd[l], ssd_norm[l], f"ssd{l}")
            mats = _s5_matrices(lam_re[l], lam_im[l], log_step[l], b_re[l], b_im[l], c_re[l], c_im[l])
            gy = _s5(u, mats, d_s5[l], f"s5{l}")

            merged = _merge(y_a, gy, w_ssd_up, w_glu, l, gates, f"merge{l}")
            h = _mm_res(merged, w_out, h, f"out_proj{l}", 1376, 512, layer=l)

            hn = _rmsnorm(h, norm_ffn[l], BF16, f"norm_ffn{l}")
            hidden = _swiglu(hn, w_gate_up, l, f"swiglu{l}")
            h = _mm_res(hidden, w_down_b, h, f"down_proj{l}", 688, 256, layer=l)
        outs.append(_rmsnorm(h, norm_final, x.dtype, "norm_final", row0=pad + N_META))
    return jnp.stack(outs, axis=0)
```
